```python
import jax, jax.numpy as jnp
from jax import lax
import numpy as np

D_MODEL = 1024
BATCH = 8
SEQ = 2048
DEPTH = 1
DEC_BATCH = 128
DEC_SEQ = 4
PAST_LEN = 8192
PAGE_SIZE = 128

HEAD_DIM = 64
A_HEADS = 8
A_KV_HEADS = 2
A_WINDOW = 128
B_GROUPS = ((128, 1), (512, 4), (2048, 16))
B_HEADS_PER_GROUP = 4
B_HEADS = 3 * B_HEADS_PER_GROUP
D_FF = 4 * D_MODEL
CONV_W = 3
BLOCK = 128
RMS_EPS = 1e-6
NEG_INF = -1e30
QA = A_HEADS * HEAD_DIM
KVA = A_KV_HEADS * HEAD_DIM
QB = B_HEADS * HEAD_DIM
B_OUT = B_HEADS_PER_GROUP * HEAD_DIM
IN_SPLITS = (QA, KVA, KVA, QB, QB, QB, D_MODEL, D_MODEL)
IN_COLS = QA + 2 * KVA + 3 * QB + 2 * D_MODEL

kernel_name = "hybrid_swa_sink_dilated_convffn_step"

F32 = jnp.float32


def rmsnorm(x, gain):
    xf = x.astype(F32)
    y = xf * lax.rsqrt(jnp.mean(xf * xf, axis=-1, keepdims=True) + RMS_EPS)
    return (y * gain.astype(F32)).astype(x.dtype)


def alibi_slopes(n_heads):
    return jnp.exp2(-8.0 * jnp.arange(1, n_heads + 1, dtype=F32) / n_heads)


def softmax_stats(s, snk):
    m = jnp.max(s, axis=-1)
    if snk is not None:
        m = jnp.maximum(m, snk)
    p = jnp.exp(s - m[..., None])
    l = jnp.sum(p, axis=-1)
    if snk is not None:
        l = l + jnp.exp(snk - m)
    return p, l, m + jnp.log(l)


def banded_attention(q, k, v, slopes, max_dist, dist_scale, sinks=None):
    n, L, hq, hd = q.shape
    hk = k.shape[2]
    rep = hq // hk
    nb = -(-L // BLOCK)
    pad = nb * BLOCK - L
    qb = jnp.pad(q, ((0, 0), (0, pad), (0, 0), (0, 0))).reshape(n, nb, BLOCK, hk, rep, hd)

    def two_blocks(t):
        tb = jnp.pad(t, ((0, 0), (BLOCK, pad), (0, 0), (0, 0))).reshape(n, nb + 1, BLOCK, hk, hd)
        return jnp.concatenate([tb[:, :-1], tb[:, 1:]], axis=2)

    k2, v2 = two_blocks(k), two_blocks(v)
    s = jnp.einsum('nbqgrd,nbkgd->nbgrqk', qb.astype(F32), k2.astype(F32)) * (HEAD_DIM ** -0.5)
    dist = (jnp.arange(BLOCK)[:, None] + BLOCK) - jnp.arange(2 * BLOCK)[None, :]
    key_pos = jnp.arange(nb)[:, None] * BLOCK + jnp.arange(2 * BLOCK)[None, :] - BLOCK
    valid = ((dist >= 0) & (dist <= max_dist))[None] & (key_pos >= 0)[:, None, :]
    bias = -slopes.astype(F32).reshape(hk, rep)[:, :, None, None] * (dist * dist_scale).astype(F32)
    s = jnp.where(valid[None, :, None, None], s + bias, NEG_INF)
    snk = None if sinks is None else sinks.astype(F32).reshape(hk, rep, 1)
    p, l, lse = softmax_stats(s, snk)
    o = jnp.einsum('nbgrqk,nbkgd->nbqgrd', p, v2.astype(F32)) / jnp.moveaxis(l, -1, 2)[..., None]
    o = o.reshape(n, nb * BLOCK, hq, hd)[:, :L]
    lse = jnp.moveaxis(lse, -1, 2).reshape(n, nb * BLOCK, hq)[:, :L]
    return o, lse


def dilated_prompt_attention(q, k, v, slopes, window, dil):
    n, L, h, hd = q.shape
    ls = L // dil

    def to_sub(t):
        return t.reshape(n, ls, dil, h, hd).transpose(0, 2, 1, 3, 4).reshape(n * dil, ls, h, hd)

    o, lse = banded_attention(to_sub(q), to_sub(k), to_sub(v), slopes, window // dil, dil)
    o = o.reshape(n, dil, ls, h, hd).transpose(0, 2, 1, 3, 4).reshape(n, L, h, hd)
    lse = lse.reshape(n, dil, ls, h).transpose(0, 2, 1, 3).reshape(n, L, h)
    return o, lse


def gathered_window_attention(q, k_buf, v_buf, k_new, v_new, slopes, max_dist, dil, sinks=None):
    n, t, hq, hd = q.shape
    hk = k_new.shape[2]
    rep = hq // hk
    wb = k_buf.shape[1]
    kf = jnp.concatenate([k_buf.astype(k_new.dtype), k_new], axis=1)
    vf = jnp.concatenate([v_buf.astype(v_new.dtype), v_new], axis=1)
    steps = jnp.arange(max_dist + 1)
    idx = wb + jnp.arange(t)[:, None] - steps[None, :] * dil
    valid = idx >= 0
    idx = jnp.maximum(idx, 0)
    kg, vg = kf[:, idx], vf[:, idx]
    qg = q.reshape(n, t, hk, rep, hd)
    s = jnp.einsum('ntgrd,ntkgd->ngrtk', qg.astype(F32), kg.astype(F32)) * (HEAD_DIM ** -0.5)
    bias = -slopes.astype(F32).reshape(hk, rep)[:, :, None, None] * (steps * dil).astype(F32)
    s = jnp.where(valid, s + bias, NEG_INF)
    snk = None if sinks is None else sinks.astype(F32).reshape(hk, rep, 1)
    p, l, lse = softmax_stats(s, snk)
    o = jnp.einsum('ngrtk,ntkgd->ntgrd', p, vg.astype(F32)) / jnp.moveaxis(l, -1, 1)[..., None]
    return o.reshape(n, t, hq, hd), jnp.moveaxis(lse, -1, 1).reshape(n, t, hq)


def window_update(prev_kv, k, v, window):
    full = jnp.concatenate([prev_kv.astype(k.dtype), jnp.stack([k, v], axis=2)], axis=1)
    keep = min(window, full.shape[1])
    return full[:, full.shape[1] - keep:]


def in_project(h, w_in):
    n, t, _ = h.shape
    cuts = np.cumsum(IN_SPLITS)[:-1].tolist()
    qa, ka, va, qb, kb, vb, ga, gb = jnp.split(jnp.einsum('ntd,dc->ntc', h, w_in), cuts, axis=-1)
    heads = lambda z, nh: z.reshape(n, t, nh, HEAD_DIM)
    return (heads(qa, A_HEADS), heads(ka, A_KV_HEADS), heads(va, A_KV_HEADS),
            heads(qb, B_HEADS), heads(kb, B_HEADS), heads(vb, B_HEADS), ga, gb)


def token_mixer(h, caches, w_in, sinks_a, w_branch_a, w_branch_b, w_out):
    n, t, _ = h.shape
    qa, ka, va, qb, kb, vb, ga, gb = in_project(h, w_in)
    slopes_a, slopes_b = alibi_slopes(A_HEADS), alibi_slopes(B_HEADS)
    if caches is None:
        prev = [jnp.zeros((n, 0, 2, A_KV_HEADS, HEAD_DIM), h.dtype)] + \
               [jnp.zeros((n, 0, 2, B_HEADS_PER_GROUP, HEAD_DIM), h.dtype)] * len(B_GROUPS)
        o_a, _ = banded_attention(qa, ka, va, slopes_a, A_WINDOW - 1, 1, sinks_a)
    else:
        prev = list(caches)
        o_a, _ = gathered_window_attention(qa, prev[0][:, :, 0], prev[0][:, :, 1], ka, va,
                                           slopes_a, A_WINDOW - 1, 1, sinks_a)
    new = [window_update(prev[0], ka, va, A_WINDOW)]
    outs, lses = [], []
    for g, (win, dil) in enumerate(B_GROUPS):
        hs = slice(g * B_HEADS_PER_GROUP, (g + 1) * B_HEADS_PER_GROUP)
        qg, kg, vg, sg = qb[:, :, hs], kb[:, :, hs], vb[:, :, hs], slopes_b[hs]
        if caches is None:
            o, lse = dilated_prompt_attention(qg, kg, vg, sg, win, dil)
        else:
            c = prev[g + 1]
            o, lse = gathered_window_attention(qg, c[:, :, 0], c[:, :, 1], kg, vg, sg, win // dil, dil)
        outs.append(o)
        lses.append(lse)
        new.append(window_update(prev[g + 1], kg, vg, win))
    wts = jax.nn.softmax(jnp.stack(lses, axis=0), axis=0)
    o_b = jnp.sum(wts[..., None] * jnp.stack(outs, axis=0), axis=0)
    y_a = jnp.einsum('ntc,cd->ntd', o_a.reshape(n, t, QA).astype(h.dtype), w_branch_a)
    y_b = jnp.einsum('ntc,cd->ntd', o_b.reshape(n, t, B_OUT).astype(h.dtype), w_branch_b)
    mixed = jax.nn.sigmoid(ga) * y_a + jax.nn.sigmoid(gb) * y_b
    return jnp.einsum('ntd,de->nte', mixed, w_out), new


def conv_ffn(h, conv_prev, w_up, conv_w, conv_b, w_down):
    t = h.shape[1]
    a, g = jnp.split(jnp.einsum('ntd,df->ntf', h, w_up), 2, axis=-1)
    ap = jnp.concatenate([conv_prev.astype(a.dtype), a], axis=1)
    c = conv_b
    for j in range(CONV_W):
        c = c + conv_w[j] * ap[:, j:j + t]
    y = jnp.einsum('ntf,fd->ntd', jax.nn.gelu(c, approximate=True) * g, w_down)
    return y, ap[:, t:]


def decoder_layer(x, caches, conv_prev, w_in, sinks_a, w_branch_a, w_branch_b, w_out,
                  g_mix_pre, g_mix_post, g_ffn_pre, g_ffn_post, w_up, conv_w, conv_b, w_down):
    mix, new_kv = token_mixer(rmsnorm(x, g_mix_pre), caches, w_in, sinks_a, w_branch_a, w_branch_b, w_out)
    x = x + rmsnorm(mix, g_mix_post)
    f, new_conv = conv_ffn(rmsnorm(x, g_ffn_pre), conv_prev, w_up, conv_w, conv_b, w_down)
    x = x + rmsnorm(f, g_ffn_post)
    return x, new_kv + [new_conv]


def setup_inputs(seed: int = 0) -> dict:
    key = jax.random.key(seed)
    ks = jax.random.split(key, 24)
    nrm = lambda k, shape, scale: jax.random.normal(k, shape, F32) * scale
    bh = B_HEADS_PER_GROUP
    return {
        "x_prompt": nrm(ks[0], (BATCH, SEQ, D_MODEL), 1.0),
        "x_sample": nrm(ks[1], (DEC_BATCH, DEC_SEQ, D_MODEL), 1.0),
        "cache_a_kv": nrm(ks[2], (DEPTH, DEC_BATCH, min(A_WINDOW, PAST_LEN), 2, A_KV_HEADS, HEAD_DIM), 1.0),
        "cache_b1_kv": nrm(ks[3], (DEPTH, DEC_BATCH, min(B_GROUPS[0][0], PAST_LEN), 2, bh, HEAD_DIM), 1.0),
        "cache_b2_kv": nrm(ks[4], (DEPTH, DEC_BATCH, min(B_GROUPS[1][0], PAST_LEN), 2, bh, HEAD_DIM), 1.0),
        "cache_b3_kv": nrm(ks[5], (DEPTH, DEC_BATCH, min(B_GROUPS[2][0], PAST_LEN), 2, bh, HEAD_DIM), 1.0),
        "state_conv": nrm(ks[6], (DEPTH, DEC_BATCH, CONV_W - 1, D_FF), 1.0),
        "w_in": nrm(ks[7], (DEPTH, D_MODEL, IN_COLS), D_MODEL ** -0.5),
        "sinks_a": nrm(ks[8], (DEPTH, A_HEADS), 0.5),
        "w_branch_a": nrm(ks[9], (DEPTH, QA, D_MODEL), QA ** -0.5),
        "w_branch_b": nrm(ks[10], (DEPTH, B_OUT, D_MODEL), B_OUT ** -0.5),
        "w_out": nrm(ks[11], (DEPTH, D_MODEL, D_MODEL), D_MODEL ** -0.5),
        "norm_mix_pre": 1.0 + nrm(ks[12], (DEPTH, D_MODEL), 0.05),
        "norm_mix_post": 1.0 + nrm(ks[13], (DEPTH, D_MODEL), 0.05),
        "norm_ffn_pre": 1.0 + nrm(ks[14], (DEPTH, D_MODEL), 0.05),
        "norm_ffn_post": 1.0 + nrm(ks[15], (DEPTH, D_MODEL), 0.05),
        "w_up": nrm(ks[16], (DEPTH, D_MODEL, 2 * D_FF), D_MODEL ** -0.5),
        "conv_w": nrm(ks[17], (DEPTH, CONV_W, D_FF), CONV_W ** -0.5),
        "conv_b": nrm(ks[18], (DEPTH, D_FF), 0.02),
        "w_down": nrm(ks[19], (DEPTH, D_FF, D_MODEL), D_FF ** -0.5),
    }


def reference(x_prompt, x_sample, cache_a_kv, cache_b1_kv, cache_b2_kv, cache_b3_kv, state_conv,
              w_in, sinks_a, w_branch_a, w_branch_b, w_out, norm_mix_pre, norm_mix_post,
              norm_ffn_pre, norm_ffn_post, w_up, conv_w, conv_b, w_down):
    yp, ys = x_prompt, x_sample
    new_p = [[] for _ in range(5)]
    new_s = [[] for _ in range(5)]
    for l in range(DEPTH):
        wts = (w_in[l], sinks_a[l], w_branch_a[l], w_branch_b[l], w_out[l], norm_mix_pre[l],
               norm_mix_post[l], norm_ffn_pre[l], norm_ffn_post[l], w_up[l], conv_w[l], conv_b[l], w_down[l])
        conv0 = jnp.zeros((yp.shape[0], CONV_W - 1, D_FF), yp.dtype)
        yp, sp = decoder_layer(yp, None, conv0, *wts)
        ys, ss = decoder_layer(ys, (cache_a_kv[l], cache_b1_kv[l], cache_b2_kv[l], cache_b3_kv[l]),
                               state_conv[l], *wts)
        for i in range(5):
            new_p[i].append(sp[i])
            new_s[i].append(ss[i])
    sp = [jnp.stack(v, axis=0) for v in new_p]
    ss = [jnp.stack(v, axis=0) for v in new_s]
    return (yp, ys, sp[0], ss[0], sp[1], ss[1], sp[2], ss[2], sp[3], ss[3], sp[4], ss[4])
```

```python
import functools

import jax
import jax.numpy as jnp
import numpy as np
from jax import lax
from jax.experimental import pallas as pl
from jax.experimental.pallas import tpu as pltpu

F32 = jnp.float32
BF16 = jnp.bfloat16

HEAD_DIM = 64
A_HEADS = 8
A_KV_HEADS = 2
A_WINDOW = 128
B_GROUPS = ((128, 1), (512, 4), (2048, 16))
B_HEADS_PER_GROUP = 4
D_MODEL = 1024
D_FF = 4096
RMS_EPS = 1e-6
NEG_INF = -1e30
Q_SCALE = HEAD_DIM ** -0.5

LANES = 128
BLOCK = 128
TOKEN_BLOCK = 512
GROUP_COLS = B_HEADS_PER_GROUP * HEAD_DIM
SAMPLE_TILE_SEQS = 32

C_QA = (0, 512)
C_KVA = (512, 768)
C_QKV1 = (768, 1536)
C_GATES = (1536, 3584)
C_QKV2 = (3584, 4352)
C_QKV3 = (4352, 5120)
R_A = (0, 256)
R_B1 = (256, 768)
R_B2 = (768, 1280)
R_B3 = (1280, 1792)

VMEM_LIMIT = 56 * 1024 * 1024

_NT = (((1,), (1,)), ((), ()))


def _const_spec(shape):
    nd = len(shape)
    return pl.BlockSpec(shape, lambda *_: (0,) * nd, pipeline_mode=pl.Buffered(1))


def _rms(x, gain):
    ms = jnp.mean(x * x, axis=-1, keepdims=True)
    return x * lax.rsqrt(ms + RMS_EPS) * gain


def _alibi_slopes(n_heads):
    return jnp.exp2(-8.0 * jnp.arange(1, n_heads + 1, dtype=F32) / n_heads)


def _prep_in_weights(w):
    qa, ka, va = w[:, 0:512], w[:, 512:640], w[:, 640:768]
    qb, kb, vb = w[:, 768:1536], w[:, 1536:2304], w[:, 2304:3072]
    gates = w[:, 3072:5120]
    g = lambda z, i: z[:, GROUP_COLS * i:GROUP_COLS * (i + 1)]
    w_rm = jnp.concatenate(
        [qa, ka, va, g(qb, 0), g(kb, 0), g(vb, 0), gates,
         g(qb, 1), g(kb, 1), g(vb, 1), g(qb, 2), g(kb, 2), g(vb, 2)], axis=1).astype(BF16)
    w_kvt = jnp.concatenate(
        [ka, va, g(kb, 0), g(vb, 0), g(kb, 1), g(vb, 1), g(kb, 2), g(vb, 2)], axis=1).T.astype(BF16)
    return w_rm, w_kvt


def _store_qkv(ref, r):
    ref[:, 0:GROUP_COLS] = (r[:, 0:GROUP_COLS] * Q_SCALE).astype(BF16)
    ref[:, GROUP_COLS:] = r[:, GROUP_COLS:].astype(BF16)


def _k1p_kernel(x_ref, g_ref, wrm_ref, wkvt_ref,
                qa_ref, kva_ref, qkv1_ref, gates_ref, qkv2_ref, qkv3_ref,
                at_ref, b1t_ref, b2t_ref, b3t_ref, slab_ref):
    tb = pl.program_id(1)
    gain = g_ref[...]

    def proj(hh, c):
        return jnp.dot(hh, wrm_ref[:, c[0]:c[1]], preferred_element_type=F32)

    h = _rms(x_ref[0], gain).astype(BF16)
    qa_ref[0] = (proj(h, C_QA) * Q_SCALE).astype(BF16)
    kva_ref[0] = proj(h, C_KVA).astype(BF16)
    _store_qkv(qkv1_ref.at[0], proj(h, C_QKV1))
    half = (C_GATES[0] + C_GATES[1]) // 2
    gates_ref[0, :, 0:D_MODEL] = proj(h, (C_GATES[0], half)).astype(BF16)
    gates_ref[0, :, D_MODEL:] = proj(h, (half, C_GATES[1])).astype(BF16)

    n_slabs = D_MODEL // LANES
    for s in range(n_slabs):
        slab_ref[s] = x_ref[0, :, s * LANES:(s + 1) * LANES]
    for dil, cols, out_ref in ((4, C_QKV2, qkv2_ref), (16, C_QKV3, qkv3_ref)):
        per = TOKEN_BLOCK // dil
        xs = jnp.concatenate(
            [jnp.concatenate([slab_ref[s, pl.ds(r, per, stride=dil), :] for r in range(dil)], axis=0)
             for s in range(n_slabs)], axis=1)
        hp = _rms(xs, gain).astype(BF16)
        rp = proj(hp, cols)
        out_ref[0, :, :, 0:GROUP_COLS] = (
            (rp[:, 0:GROUP_COLS] * Q_SCALE).astype(BF16).reshape(dil, per, GROUP_COLS))
        out_ref[0, :, :, GROUP_COLS:] = (
            rp[:, GROUP_COLS:].astype(BF16).reshape(dil, per, 2 * GROUP_COLS))

    b3t_ref[0] = lax.dot_general(wkvt_ref[R_B3[0]:R_B3[1], :], h, _NT, preferred_element_type=F32)

    @pl.when(tb == pl.num_programs(1) - 1)
    def _():
        b2t_ref[0] = lax.dot_general(wkvt_ref[R_B2[0]:R_B2[1], :], h, _NT, preferred_element_type=F32)
        t = lax.dot_general(wkvt_ref[R_A[0]:R_B1[1], :], h[TOKEN_BLOCK - BLOCK:, :], _NT,
                            preferred_element_type=F32)
        at_ref[0] = t[R_A[0]:R_A[1]]
        b1t_ref[0] = t[R_B1[0]:R_B1[1]]


def _in_proj_prompt(x, gain, w_rm, w_kvt):
    n, s, d = x.shape
    assert s % TOKEN_BLOCK == 0 and TOKEN_BLOCK == B_GROUPS[1][0] and s == B_GROUPS[2][0]
    nb = s // TOKEN_BLOCK
    out_shape = (
        jax.ShapeDtypeStruct((n, s, 512), BF16),
        jax.ShapeDtypeStruct((n, s, 256), BF16),
        jax.ShapeDtypeStruct((n, s, 768), BF16),
        jax.ShapeDtypeStruct((n, s, 2 * D_MODEL), BF16),
        jax.ShapeDtypeStruct((n, 4, s // 4, 768), BF16),
        jax.ShapeDtypeStruct((n, 16, s // 16, 768), BF16),
        jax.ShapeDtypeStruct((n, 256, BLOCK), F32),
        jax.ShapeDtypeStruct((n, 512, BLOCK), F32),
        jax.ShapeDtypeStruct((n, 512, TOKEN_BLOCK), F32),
        jax.ShapeDtypeStruct((n, 512, s), F32),
    )
    tok = lambda w: pl.BlockSpec((1, TOKEN_BLOCK, w), lambda i, j: (i, j, 0))
    out_specs = (
        tok(512), tok(256), tok(768), tok(2 * D_MODEL),
        pl.BlockSpec((1, 4, TOKEN_BLOCK // 4, 768), lambda i, j: (i, 0, j, 0)),
        pl.BlockSpec((1, 16, TOKEN_BLOCK // 16, 768), lambda i, j: (i, 0, j, 0)),
        pl.BlockSpec((1, 256, BLOCK), lambda i, j: (i, 0, 0)),
        pl.BlockSpec((1, 512, BLOCK), lambda i, j: (i, 0, 0)),
        pl.BlockSpec((1, 512, TOKEN_BLOCK), lambda i, j: (i, 0, 0)),
        pl.BlockSpec((1, 512, TOKEN_BLOCK), lambda i, j: (i, 0, j)),
    )
    return pl.pallas_call(
        _k1p_kernel,
        grid=(n, nb),
        in_specs=[tok(d), _const_spec((1, d)), _const_spec(w_rm.shape), _const_spec(w_kvt.shape)],
        out_specs=out_specs,
        out_shape=out_shape,
        scratch_shapes=[pltpu.VMEM((d // LANES, TOKEN_BLOCK, LANES), F32)],
        compiler_params=pltpu.CompilerParams(
            dimension_semantics=("arbitrary", "arbitrary"), vmem_limit_bytes=VMEM_LIMIT),
        name="in_proj_prompt",
    )(x, gain, w_rm, w_kvt)


def _prompt_bias_tables():
    i = jnp.arange(BLOCK)[:, None]
    j = jnp.arange(2 * BLOCK)[None, :]
    dist = (i + BLOCK) - j
    tabs = []
    slopes_a, slopes_b = _alibi_slopes(A_HEADS), _alibi_slopes(3 * B_HEADS_PER_GROUP)
    valid_a = (dist >= 0) & (dist <= A_WINDOW - 1)
    for h in range(A_HEADS):
        tabs.append(jnp.where(valid_a, -slopes_a[h] * dist.astype(F32), NEG_INF))
    for g, (win, dil) in enumerate(B_GROUPS):
        valid = (dist >= 0) & (dist <= win // dil)
        for hh in range(B_HEADS_PER_GROUP):
            s = slopes_b[g * B_HEADS_PER_GROUP + hh]
            tabs.append(jnp.where(valid, -s * (dist * dil).astype(F32), NEG_INF))
    return jnp.stack(tabs, axis=0).astype(F32)


def _band_head(q, kc, vc, kp, vp, bias, no_prev, sink):
    sc = lax.dot_general(q, kc, _NT, preferred_element_type=F32) + bias[:, BLOCK:]
    m = jnp.max(sc, axis=-1, keepdims=True)
    if kp is not None:
        bp = jnp.where(no_prev, NEG_INF, bias[:, :BLOCK])
        sp = lax.dot_general(q, kp, _NT, preferred_element_type=F32) + bp
        m = jnp.maximum(m, jnp.max(sp, axis=-1, keepdims=True))
    if sink is not None:
        m = jnp.maximum(m, sink)
    pc = jnp.exp(sc - m)
    l = jnp.sum(pc, axis=-1, keepdims=True)
    o = jnp.dot(pc.astype(BF16), vc, preferred_element_type=F32)
    if kp is not None:
        pp = jnp.exp(sp - m)
        l = l + jnp.sum(pp, axis=-1, keepdims=True)
        o = o + jnp.dot(pp.astype(BF16), vp, preferred_element_type=F32)
    if sink is not None:
        l = l + jnp.exp(sink - m)
    return o / l, m + jnp.log(l)


def _k2p_kernel(sink_ref, bias_ref,
                qa_ref, kva_ref, kvap_ref,
                q1_ref, q1p_ref, q2_ref, q2p_ref, q3_ref,
                oa_ref, og1_ref, og2_ref, og3_ref):
    j = pl.program_id(1)
    hd = HEAD_DIM
    first_blk = j == 0
    first_sub = (j % 4) == 0

    outs = []
    for h in range(A_HEADS):
        g = h // (A_HEADS // A_KV_HEADS)
        q = qa_ref[0, :, h * hd:(h + 1) * hd]
        kc = kva_ref[0, :, g * hd:(g + 1) * hd]
        vc = kva_ref[0, :, 128 + g * hd:128 + (g + 1) * hd]
        kp = kvap_ref[0, :, g * hd:(g + 1) * hd]
        vp = kvap_ref[0, :, 128 + g * hd:128 + (g + 1) * hd]
        o, _ = _band_head(q, kc, vc, kp, vp, bias_ref[h], first_blk, sink_ref[0, h])
        outs.append(o)
    oa_ref[0] = jnp.concatenate(outs, axis=-1).astype(BF16)

    def group(cur, prev, base, no_prev, out_ref):
        os_, ls_ = [], []
        for h in range(B_HEADS_PER_GROUP):
            sl = lambda ref, part: ref[:, part * GROUP_COLS + h * hd:part * GROUP_COLS + (h + 1) * hd]
            kp = vp = None
            if prev is not None:
                kp, vp = sl(prev, 1), sl(prev, 2)
            o, lse = _band_head(sl(cur, 0), sl(cur, 1), sl(cur, 2), kp, vp,
                                bias_ref[base + h], no_prev, None)
            os_.append(o)
            ls_.append(jnp.broadcast_to(lse, (BLOCK, hd)))
        out_ref[...] = jnp.concatenate(os_ + ls_, axis=-1)

    group(q1_ref.at[0], q1p_ref.at[0], A_HEADS, first_blk, og1_ref.at[0])
    group(q2_ref.at[0, 0], q2p_ref.at[0, 0], A_HEADS + 4, first_sub, og2_ref.at[0, 0])
    group(q3_ref.at[0, 0], None, A_HEADS + 8, None, og3_ref.at[0, 0])


def _attention_prompt(sinks, qa, kva, qkv1, qkv2, qkv3):
    n, s, _ = qa.shape
    nb = s // BLOCK
    assert nb == 16 and qkv2.shape[2] // BLOCK == 4 and qkv3.shape[2] == BLOCK
    bias = _prompt_bias_tables()
    prev = lambda j: jnp.maximum(j - 1, 0)
    in_specs = [
        pl.BlockSpec(memory_space=pltpu.SMEM),
        _const_spec(bias.shape),
        pl.BlockSpec((1, BLOCK, 512), lambda i, j: (i, j, 0)),
        pl.BlockSpec((1, BLOCK, 256), lambda i, j: (i, j, 0)),
        pl.BlockSpec((1, BLOCK, 256), lambda i, j: (i, prev(j), 0)),
        pl.BlockSpec((1, BLOCK, 768), lambda i, j: (i, j, 0)),
        pl.BlockSpec((1, BLOCK, 768), lambda i, j: (i, prev(j), 0)),
        pl.BlockSpec((1, 1, BLOCK, 768), lambda i, j: (i, j // 4, j % 4, 0)),
        pl.BlockSpec((1, 1, BLOCK, 768), lambda i, j: (i, j // 4, prev(j % 4), 0)),
        pl.BlockSpec((1, 1, BLOCK, 768), lambda i, j: (i, j, 0, 0)),
    ]
    out_shape = (
        jax.ShapeDtypeStruct((n, s, 512), BF16),
        jax.ShapeDtypeStruct((n, s, 512), F32),
        jax.ShapeDtypeStruct((n, 4, s // 4, 512), F32),
        jax.ShapeDtypeStruct((n, 16, s // 16, 512), F32),
    )
    out_specs = (
        pl.BlockSpec((1, BLOCK, 512), lambda i, j: (i, j, 0)),
        pl.BlockSpec((1, BLOCK, 512), lambda i, j: (i, j, 0)),
        pl.BlockSpec((1, 1, BLOCK, 512), lambda i, j: (i, j // 4, j % 4, 0)),
        pl.BlockSpec((1, 1, BLOCK, 512), lambda i, j: (i, j, 0, 0)),
    )
    return pl.pallas_call(
        _k2p_kernel,
        grid=(n, nb),
        in_specs=in_specs,
        out_specs=out_specs,
        out_shape=out_shape,
        compiler_params=pltpu.CompilerParams(
            dimension_semantics=("arbitrary", "arbitrary"), vmem_limit_bytes=VMEM_LIMIT),
        name="attention_prompt",
    )(sinks, bias, qa, kva, kva, qkv1, qkv1, qkv2, qkv2, qkv3)


def _sigmoid(x):
    return 1.0 / (1.0 + jnp.exp(-x))


def _mix_tail(oa, ob, gates_ref, x_ref, wba_ref, wbb_ref, wout_ref, gpost_ref, out_ref):
    ya = jnp.dot(oa, wba_ref[...], preferred_element_type=F32)
    yb = jnp.dot(ob, wbb_ref[...], preferred_element_type=F32)
    ga = gates_ref[:, 0:D_MODEL].astype(F32)
    gb = gates_ref[:, D_MODEL:].astype(F32)
    mixed = _sigmoid(ga) * ya + _sigmoid(gb) * yb
    mix = jnp.dot(mixed.astype(BF16), wout_ref[...], preferred_element_type=F32)
    out_ref[...] = x_ref[...] + _rms(mix, gpost_ref[...])


def _k3p_kernel(oa_ref, og1_ref, og2_ref, og3_ref, gates_ref, x_ref,
                wba_ref, wbb_ref, wout_ref, gpost_ref, out_ref, s2_ref, s3_ref):
    n_slabs = 2 * GROUP_COLS // LANES
    for dil, src, dst in ((4, og2_ref, s2_ref), (16, og3_ref, s3_ref)):
        for r in range(dil):
            for s in range(n_slabs):
                dst[s, pl.ds(r, TOKEN_BLOCK // dil, stride=dil), :] = (
                    src[0, r, :, s * LANES:(s + 1) * LANES])
    o1, l1 = og1_ref[0, :, 0:GROUP_COLS], og1_ref[0, :, GROUP_COLS:]
    o2 = jnp.concatenate([s2_ref[0], s2_ref[1]], axis=1)
    l2 = jnp.concatenate([s2_ref[2], s2_ref[3]], axis=1)
    o3 = jnp.concatenate([s3_ref[0], s3_ref[1]], axis=1)
    l3 = jnp.concatenate([s3_ref[2], s3_ref[3]], axis=1)
    m = jnp.maximum(jnp.maximum(l1, l2), l3)
    e1, e2, e3 = jnp.exp(l1 - m), jnp.exp(l2 - m), jnp.exp(l3 - m)
    ob = (e1 * o1 + e2 * o2 + e3 * o3) / (e1 + e2 + e3)
    _mix_tail(oa_ref[0], ob.astype(BF16), gates_ref.at[0], x_ref.at[0],
              wba_ref, wbb_ref, wout_ref, gpost_ref, out_ref.at[0])


def _out_proj_prompt(oa, og1, og2, og3, gates, x, wba, wbb, wout, gpost):
    n, s, d = x.shape
    nb = s // TOKEN_BLOCK
    tok = lambda w: pl.BlockSpec((1, TOKEN_BLOCK, w), lambda i, j: (i, j, 0))
    return pl.pallas_call(
        _k3p_kernel,
        grid=(n, nb),
        in_specs=[
            tok(512), tok(512),
            pl.BlockSpec((1, 4, TOKEN_BLOCK // 4, 512), lambda i, j: (i, 0, j, 0)),
            pl.BlockSpec((1, 16, TOKEN_BLOCK // 16, 512), lambda i, j: (i, 0, j, 0)),
            tok(2 * D_MODEL), tok(d),
            _const_spec(wba.shape), _const_spec(wbb.shape), _const_spec(wout.shape),
            _const_spec(gpost.shape),
        ],
        out_specs=tok(d),
        out_shape=jax.ShapeDtypeStruct((n, s, d), F32),
        scratch_shapes=[pltpu.VMEM((2 * GROUP_COLS // LANES, TOKEN_BLOCK, LANES), F32)] * 2,
        compiler_params=pltpu.CompilerParams(
            dimension_semantics=("arbitrary", "arbitrary"), vmem_limit_bytes=VMEM_LIMIT),
        name="out_proj_prompt",
    )(oa, og1, og2, og3, gates, x, wba, wbb, wout, gpost)


def _k3s_kernel(o_ref, gates_ref, x_ref, wba_ref, wbb_ref, wout_ref, gpost_ref, out_ref):
    oa = o_ref[:, 0:512].astype(BF16)
    ob = o_ref[:, 512:768].astype(BF16)
    _mix_tail(oa, ob, gates_ref, x_ref, wba_ref, wbb_ref, wout_ref, gpost_ref, out_ref)


def _out_proj_sample(o, gates, x, wba, wbb, wout, gpost):
    t, d = x.shape
    full = lambda a: pl.BlockSpec(a.shape, lambda i: (0,) * a.ndim)
    args = (o, gates, x, wba, wbb, wout, gpost)
    return pl.pallas_call(
        _k3s_kernel,
        grid=(1,),
        in_specs=[full(a) for a in args],
        out_specs=pl.BlockSpec((t, d), lambda i: (0, 0)),
        out_shape=jax.ShapeDtypeStruct((t, d), F32),
        compiler_params=pltpu.CompilerParams(
            dimension_semantics=("arbitrary",), vmem_limit_bytes=VMEM_LIMIT),
        name="out_proj_sample",
    )(*args)


FF_CHUNK = 512


def _gelu_tanh(c):
    return 0.5 * c * (1.0 + jnp.tanh(np.sqrt(2.0 / np.pi).astype(np.float32)
                                     * (c + 0.044715 * (c * c * c))))


def _ffn_chunk(h, w_a, w_g, cw, cb, w_down, prev_fn):
    a = jnp.dot(h, w_a, preferred_element_type=F32)
    g = jnp.dot(h, w_g, preferred_element_type=F32)
    p1, p2 = prev_fn(a)
    conv = cb + cw[0:1] * p2 + cw[1:2] * p1 + cw[2:3] * a
    act = (_gelu_tanh(conv) * g).astype(BF16)
    return jnp.dot(act, w_down, preferred_element_type=F32), a


def _k4p_kernel(x_ref, gpre_ref, gpost_ref, wup_ref, cw_ref, cb_ref, wdown_ref,
                out_ref, tail_ref):
    tb = pl.program_id(1)

    @pl.when(tb == 0)
    def _():
        tail_ref[...] = jnp.zeros_like(tail_ref)

    rows = lax.broadcasted_iota(jnp.int32, (TOKEN_BLOCK, FF_CHUNK), 0)
    x = x_ref[0]
    h = _rms(x, gpre_ref[...]).astype(BF16)
    y = jnp.zeros((TOKEN_BLOCK, D_MODEL), F32)
    for c0 in range(0, D_FF, FF_CHUNK):
        c1 = c0 + FF_CHUNK

        def prev(a, c0=c0, c1=c1):
            t6, t7 = tail_ref[0, 6:7, c0:c1], tail_ref[0, 7:8, c0:c1]
            p1 = jnp.where(rows == 0, t7, pltpu.roll(a, 1, axis=0))
            p2 = jnp.where(rows == 0, t6, jnp.where(rows == 1, t7, pltpu.roll(a, 2, axis=0)))
            return p1, p2

        yc, a = _ffn_chunk(h, wup_ref[:, c0:c1], wup_ref[:, D_FF + c0:D_FF + c1],
                           cw_ref[:, c0:c1], cb_ref[:, c0:c1], wdown_ref[c0:c1, :], prev)
        tail_ref[0, :, c0:c1] = a[TOKEN_BLOCK - 8:, :]
        y = y + yc
    out_ref[0] = x + _rms(y, gpost_ref[...])


def _ffn_prompt(x, gpre, gpost, wup, cw, cb, wdown):
    n, s, d = x.shape
    nb = s // TOKEN_BLOCK
    tok = pl.BlockSpec((1, TOKEN_BLOCK, d), lambda i, j: (i, j, 0))
    consts = (gpre, gpost, wup, cw, cb, wdown)
    return pl.pallas_call(
        _k4p_kernel,
        grid=(n, nb),
        in_specs=[tok] + [_const_spec(a.shape) for a in consts],
        out_specs=(tok, pl.BlockSpec((1, 8, D_FF), lambda i, j: (i, 0, 0))),
        out_shape=(jax.ShapeDtypeStruct((n, s, d), F32), jax.ShapeDtypeStruct((n, 8, D_FF), F32)),
        compiler_params=pltpu.CompilerParams(
            dimension_semantics=("arbitrary", "arbitrary"), vmem_limit_bytes=VMEM_LIMIT),
        name="ffn_prompt",
    )(x, *consts)


def _k4s_kernel(x_ref, e1_ref, e2_ref, gpre_ref, gpost_ref, wa_ref, wg_ref, cw_ref, cb_ref,
                wdown_ref, out_ref, a_ref, h_ref, y_ref):
    c = pl.program_id(0)

    @pl.when(c == 0)
    def _():
        h_ref[...] = _rms(x_ref[...], gpre_ref[...]).astype(BF16)
        y_ref[...] = jnp.zeros_like(y_ref)

    pos = lax.broadcasted_iota(jnp.int32, a_ref.shape, 0) % 4

    def prev(a):
        return (jnp.where(pos == 0, e1_ref[...], pltpu.roll(a, 1, axis=0)),
                jnp.where(pos < 2, e2_ref[...], pltpu.roll(a, 2, axis=0)))

    yc, a = _ffn_chunk(h_ref[...], wa_ref[...], wg_ref[...], cw_ref[...], cb_ref[...],
                       wdown_ref[...], prev)
    a_ref[...] = a
    y_ref[...] += yc

    @pl.when(c == pl.num_programs(0) - 1)
    def _():
        out_ref[...] = x_ref[...] + _rms(y_ref[...], gpost_ref[...])


def _ffn_sample(x, e1, e2, gpre, gpost, wup, cw, cb, wdown):
    t, d = x.shape
    nc = D_FF // FF_CHUNK
    const = lambda a: pl.BlockSpec(a.shape, lambda c: (0,) * a.ndim)
    cols = lambda rows: pl.BlockSpec((rows, FF_CHUNK), lambda c: (0, c))
    return pl.pallas_call(
        _k4s_kernel,
        grid=(nc,),
        in_specs=[const(x), cols(t), cols(t), const(gpre), const(gpost),
                  cols(d), pl.BlockSpec((d, FF_CHUNK), lambda c: (0, nc + c)),
                  cols(3), cols(1), pl.BlockSpec((FF_CHUNK, d), lambda c: (c, 0))],
        out_specs=(pl.BlockSpec((t, d), lambda c: (0, 0)), cols(t)),
        out_shape=(jax.ShapeDtypeStruct((t, d), F32), jax.ShapeDtypeStruct((t, D_FF), F32)),
        scratch_shapes=[pltpu.VMEM((t, d), BF16), pltpu.VMEM((t, d), F32)],
        compiler_params=pltpu.CompilerParams(
            dimension_semantics=("arbitrary",), vmem_limit_bytes=VMEM_LIMIT),
        name="ffn_sample",
    )(x, e1, e2, gpre, gpost, wup, wup, cw, cb, wdown)


def _k1s_kernel(x_ref, g_ref, wrm_ref, wkvt_ref, rep_ref, qbd_ref, kvt_ref, gates_ref):
    h = _rms(x_ref[...], g_ref[...]).astype(BF16)
    shape = (rep_ref.shape[0], GROUP_COLS)
    own = ((lax.broadcasted_iota(jnp.int32, shape, 0) % 16) // 4
           == lax.broadcasted_iota(jnp.int32, shape, 1) // HEAD_DIM)
    starts = (C_QA[0], C_QA[0] + GROUP_COLS, C_QKV1[0], C_QKV2[0], C_QKV3[0])
    for u, c0 in enumerate(starts):
        q = jnp.dot(h, wrm_ref[:, c0:c0 + GROUP_COLS], preferred_element_type=F32)
        q = (q * Q_SCALE).astype(BF16)
        qrep = jnp.dot(rep_ref[...], q, preferred_element_type=F32)
        qbd_ref[:, u * GROUP_COLS:(u + 1) * GROUP_COLS] = jnp.where(own, qrep, 0.0).astype(BF16)
    kvt_ref[...] = lax.dot_general(wkvt_ref[...], h, _NT, preferred_element_type=F32)
    half = (C_GATES[0] + C_GATES[1]) // 2
    gates_ref[:, 0:D_MODEL] = jnp.dot(h, wrm_ref[:, C_GATES[0]:half],
                                      preferred_element_type=F32).astype(BF16)
    gates_ref[:, D_MODEL:] = jnp.dot(h, wrm_ref[:, half:C_GATES[1]],
                                     preferred_element_type=F32).astype(BF16)


def _in_proj_sample(x, gain, w_rm, w_kvt):
    t, d = x.shape
    r = jnp.arange(4 * t)
    src = (r // 16) * 4 + r % 4
    rep = (src[:, None] == jnp.arange(t)[None, :]).astype(BF16)
    args = (x, gain, w_rm, w_kvt, rep)
    full = lambda a: pl.BlockSpec(a.shape, lambda i: (0,) * a.ndim, pipeline_mode=pl.Buffered(1))
    out_shape = (
        jax.ShapeDtypeStruct((4 * t, 1280), BF16),
        jax.ShapeDtypeStruct((w_kvt.shape[0], t), F32),
        jax.ShapeDtypeStruct((t, 2 * D_MODEL), BF16),
    )
    return pl.pallas_call(
        _k1s_kernel,
        grid=(1,),
        in_specs=[full(a) for a in args],
        out_specs=tuple(pl.BlockSpec(s.shape, lambda i: (0, 0)) for s in out_shape),
        out_shape=out_shape,
        compiler_params=pltpu.CompilerParams(
            dimension_semantics=("arbitrary",), vmem_limit_bytes=VMEM_LIMIT),
        name="in_proj_sample",
    )(*args)


_UNIT_COLS = ((0, 256), (256, 512), (512, 768), (768, 1024), (1024, 1280))
_UNIT_W = (A_WINDOW, A_WINDOW, B_GROUPS[0][0], B_GROUPS[1][0], B_GROUPS[2][0])
_UNIT_BIAS_OFF = tuple(int(v) for v in np.cumsum((0,) + _UNIT_W)[:-1])


def _sample_tables(sinks):
    slopes_a, slopes_b = _alibi_slopes(A_HEADS), _alibi_slopes(3 * B_HEADS_PER_GROUP)
    row = jnp.arange(16)
    slot, tok = row // 4, row % 4
    unit_slopes = [slopes_a[slot], slopes_a[4 + slot],
                   slopes_b[slot], slopes_b[4 + slot], slopes_b[8 + slot]]
    dils = (1, 1, 1, 4, 16)
    maxd = (A_WINDOW - 1, A_WINDOW - 1) + tuple(w for w, _ in B_GROUPS)
    bias = []
    for u in range(5):
        w = _UNIT_W[u]
        dist = (w + tok)[:, None] - jnp.arange(w)[None, :]
        valid = (dist % dils[u] == 0) & (dist <= maxd[u])
        bias.append(jnp.where(valid, -unit_slopes[u][:, None] * dist.astype(F32), NEG_INF))
    bias = jnp.concatenate(bias, axis=1).astype(F32)
    slope = jnp.broadcast_to(jnp.stack(unit_slopes)[:, :, None], (5, 16, 128)).astype(F32)
    sink = jnp.stack([sinks[slot], sinks[4 + slot]])
    sink = jnp.broadcast_to(sink[:, :, None], (2, 16, 128)).astype(F32)
    return bias, slope, sink


def _k2s_kernel(qbd_ref, new_ref, ca_ref, cb1_ref, cb2_ref, cb3_ref, bias_ref, slope_ref, sink_ref,
                na_ref, nb1_ref, nb2_ref, nb3_ref, o_ref):
    nl = pl.program_id(0) % SAMPLE_TILE_SEQS
    row = lax.broadcasted_iota(jnp.int32, (16, 128), 0)
    lane = lax.broadcasted_iota(jnp.int32, (16, 128), 1)
    tok, ltok = row % 4, lane % 4
    mine = (lane // 4) == nl
    dist_new = (tok - ltok).astype(F32)
    own = (lax.broadcasted_iota(jnp.int32, (16, GROUP_COLS), 0) // 4
           == lax.broadcasted_iota(jnp.int32, (16, GROUP_COLS), 1) // HEAD_DIM)
    pick = (lax.broadcasted_iota(jnp.int32, (8, 16), 1) % 4
            == lax.broadcasted_iota(jnp.int32, (8, 16), 0)).astype(BF16)

    def scores(u, kt, knew):
        q = qbd_ref[:, _UNIT_COLS[u][0]:_UNIT_COLS[u][1]]
        w = _UNIT_W[u]
        off = _UNIT_BIAS_OFF[u]
        s = jnp.dot(q, kt.astype(BF16), preferred_element_type=F32) + bias_ref[:, off:off + w]
        ok = mine & ((ltok <= tok) if u < 3 else (ltok == tok))
        bn = jnp.where(ok, -slope_ref[u] * dist_new, NEG_INF)
        sn = jnp.dot(q, knew.astype(BF16), preferred_element_type=F32) + bn
        m = jnp.maximum(jnp.max(s, axis=-1, keepdims=True), jnp.max(sn, axis=-1, keepdims=True))
        return s, sn, m

    def weighted(s, sn, m, vt, vnew):
        p, pn = jnp.exp(s - m), jnp.exp(sn - m)
        l = jnp.sum(p, axis=-1, keepdims=True) + jnp.sum(pn, axis=-1, keepdims=True)
        o = (lax.dot_general(p.astype(BF16), vt.astype(BF16), _NT, preferred_element_type=F32)
             + lax.dot_general(pn.astype(BF16), vnew.astype(BF16), _NT, preferred_element_type=F32))
        return o, l

    def finish(o, l):
        o = jnp.where(own, o / l, 0.0).astype(BF16)
        return jnp.dot(pick, o, preferred_element_type=F32)

    tile4 = lambda a: jnp.concatenate([a, a, a, a], axis=0)
    outs = []
    for g in range(A_KV_HEADS):
        kt = tile4(ca_ref[0, g * 64:(g + 1) * 64, :])
        vt = tile4(ca_ref[0, 128 + g * 64:128 + (g + 1) * 64, :])
        knew = tile4(new_ref[R_A[0] + g * 64:R_A[0] + (g + 1) * 64, :])
        vnew = tile4(new_ref[R_A[0] + 128 + g * 64:R_A[0] + 128 + (g + 1) * 64, :])
        s, sn, m = scores(g, kt, knew)
        sink = sink_ref[g][:, 0:1]
        m = jnp.maximum(m, sink)
        o, l = weighted(s, sn, m, vt, vnew)
        outs.append(finish(o, l + jnp.exp(sink - m)))

    parts = []
    for u, (c_ref, rows) in ((2, (cb1_ref, R_B1)), (3, (cb2_ref, R_B2)), (4, (cb3_ref, R_B3))):
        kt, vt = c_ref[0, 0:GROUP_COLS, :], c_ref[0, GROUP_COLS:, :]
        knew = new_ref[rows[0]:rows[0] + GROUP_COLS, :]
        vnew = new_ref[rows[0] + GROUP_COLS:rows[1], :]
        parts.append(scores(u, kt, knew) + (vt, vnew))
    m = functools.reduce(jnp.maximum, [p[2] for p in parts])
    o_sum, l_sum = 0.0, 0.0
    for s, sn, _, vt, vnew in parts:
        o, l = weighted(s, sn, m, vt, vnew)
        o_sum, l_sum = o_sum + o, l_sum + l
    outs.append(finish(o_sum, l_sum))
    o_ref[0] = jnp.concatenate(outs, axis=-1)

    lane_t = lax.broadcasted_iota(jnp.int32, (8, 128), 1)
    for c_ref, n_ref, rows in ((ca_ref, na_ref, R_A), (cb1_ref, nb1_ref, R_B1),
                               (cb2_ref, nb2_ref, R_B2), (cb3_ref, nb3_ref, R_B3)):
        w = c_ref.shape[2]
        nrows = rows[1] - rows[0]
        chunk = 64

        def body(i, carry, c_ref=c_ref, n_ref=n_ref, rows=rows, w=w):
            r0 = pl.multiple_of(i * chunk, chunk)
            x = c_ref[0, pl.ds(r0, chunk), :]
            shifted = pltpu.roll(x, w - 4, axis=1)
            new = new_ref[pl.ds(pl.multiple_of(rows[0] + r0, chunk), chunk), :]
            new = pltpu.roll(new, 124 - 4 * nl, axis=1)
            if w > 128:
                n_ref[0, pl.ds(r0, chunk), 0:w - 128] = shifted[:, 0:w - 128]
            sel = jnp.broadcast_to(lane_t[0:1] >= 124, (chunk, 128))
            n_ref[0, pl.ds(r0, chunk), w - 128:w] = jnp.where(sel, new, shifted[:, w - 128:w])
            return carry

        lax.fori_loop(0, nrows // chunk, body, 0)


def _attention_sample(qbd, kvt_new, ca, cb1, cb2, cb3, sinks):
    nseq = ca.shape[0]
    assert nseq % SAMPLE_TILE_SEQS == 0 and kvt_new.shape[1] == 4 * nseq
    assert (ca.shape[2], cb1.shape[2], cb2.shape[2], cb3.shape[2]) == _UNIT_W[1:]
    bias, slope, sink = _sample_tables(sinks)
    cache_spec = lambda a: pl.BlockSpec((1,) + a.shape[1:], lambda i: (i, 0, 0))
    in_specs = [
        pl.BlockSpec((16, 1280), lambda i: (i, 0)),
        pl.BlockSpec((kvt_new.shape[0], 128), lambda i: (0, i // SAMPLE_TILE_SEQS)),
        cache_spec(ca), cache_spec(cb1), cache_spec(cb2), cache_spec(cb3),
        _const_spec(bias.shape), _const_spec(slope.shape), _const_spec(sink.shape),
    ]
    out_shape = tuple(jax.ShapeDtypeStruct(a.shape, F32) for a in (ca, cb1, cb2, cb3)) + (
        jax.ShapeDtypeStruct((nseq, 8, 768), F32),)
    out_specs = tuple(cache_spec(a) for a in (ca, cb1, cb2, cb3)) + (
        pl.BlockSpec((1, 8, 768), lambda i: (i, 0, 0)),)
    return pl.pallas_call(
        _k2s_kernel,
        grid=(nseq,),
        in_specs=in_specs,
        out_specs=out_specs,
        out_shape=out_shape,
        compiler_params=pltpu.CompilerParams(
            dimension_semantics=("arbitrary",), vmem_limit_bytes=VMEM_LIMIT),
        name="attention_sample",
    )(qbd, kvt_new, ca, cb1, cb2, cb3, bias, slope, sink)


def _to_feature_major(cache):
    n, w = cache.shape[0], cache.shape[1]
    return jnp.transpose(cache, (0, 2, 3, 4, 1)).reshape(n, -1, w)


def _from_feature_major(c, heads):
    n, _, w = c.shape
    return jnp.transpose(c.reshape(n, 2, heads, HEAD_DIM, w), (0, 4, 1, 2, 3))[None]


def kernel(x_prompt, x_sample, cache_a_kv, cache_b1_kv, cache_b2_kv, cache_b3_kv, state_conv,
           w_in, sinks_a, w_branch_a, w_branch_b, w_out, norm_mix_pre, norm_mix_post,
           norm_ffn_pre, norm_ffn_post, w_up, conv_w, conv_b, w_down):
    assert w_in.shape[0] == 1, "single layer"
    w_rm, w_kvt = _prep_in_weights(w_in[0])
    wba, wbb, wout = (w_branch_a[0].astype(BF16), w_branch_b[0].astype(BF16), w_out[0].astype(BF16))
    wup, wdown = w_up[0].astype(BF16), w_down[0].astype(BF16)
    g_pre, g_post, gf_pre, gf_post = norm_mix_pre, norm_mix_post, norm_ffn_pre, norm_ffn_post
    cw, cb = conv_w[0], conv_b

    qa, kva, qkv1, gates, qkv2, qkv3, at, b1t, b2t, b3t = _in_proj_prompt(x_prompt, g_pre, w_rm, w_kvt)
    oa, og1, og2, og3 = _attention_prompt(sinks_a, qa, kva, qkv1, qkv2, qkv3)
    x1 = _out_proj_prompt(oa, og1, og2, og3, gates, x_prompt, wba, wbb, wout, g_post)
    y_prompt, tail = _ffn_prompt(x1, gf_pre, gf_post, wup, cw, cb, wdown)
    new_conv_p = tail[:, 6:8, :][None]

    nseq, tnew, d = x_sample.shape
    assert tnew == 4
    xs = x_sample.reshape(nseq * tnew, d)
    qbd, kvt_new, gates_s = _in_proj_sample(xs, g_pre, w_rm, w_kvt)
    ca, cb1, cb2, cb3 = (_to_feature_major(c[0]) for c in
                         (cache_a_kv, cache_b1_kv, cache_b2_kv, cache_b3_kv))
    na, nb1, nb2, nb3, o_s = _attention_sample(qbd, kvt_new, ca, cb1, cb2, cb3, sinks_a[0])
    o_s = o_s[:, 0:4, :].reshape(nseq * tnew, 768)
    x1s = _out_proj_sample(o_s, gates_s, xs, wba, wbb, wout, g_post)
    st = state_conv[0]
    zero = jnp.zeros_like(st[:, 0])
    e1 = jnp.stack([st[:, 1], zero, zero, zero], axis=1).reshape(nseq * tnew, D_FF)
    e2 = jnp.stack([st[:, 0], st[:, 1], zero, zero], axis=1).reshape(nseq * tnew, D_FF)
    ys, a_s = _ffn_sample(x1s, e1, e2, gf_pre, gf_post, wup, cw, cb, wdown)
    y_sample = ys.reshape(nseq, tnew, d)
    new_conv_s = a_s.reshape(nseq, tnew, D_FF)[:, 2:4, :][None]

    return (y_prompt, y_sample,
            _from_feature_major(at, A_KV_HEADS), _from_feature_major(na, A_KV_HEADS),
            _from_feature_major(b1t, 4), _from_feature_major(nb1, 4),
            _from_feature_major(b2t, 4), _from_feature_major(nb2, 4),
            _from_feature_major(b3t, 4), _from_feature_major(nb3, 4),
            new_conv_p, new_conv_s)
```

```python
import functools

import jax
import jax.numpy as jnp
import numpy as np
from jax import lax
from jax.experimental import pallas as pl
from jax.experimental.pallas import tpu as pltpu

F32 = jnp.float32
BF16 = jnp.bfloat16

HEAD_DIM = 64
A_HEADS = 8
A_KV_HEADS = 2
A_WINDOW = 128
B_GROUPS = ((128, 1), (512, 4), (2048, 16))
B_HEADS_PER_GROUP = 4
D_MODEL = 1024
D_FF = 4096
RMS_EPS = 1e-6
NEG_INF = -1e30
Q_SCALE = HEAD_DIM ** -0.5

LANES = 128
BLOCK = 128
TOKEN_BLOCK = 512
GROUP_COLS = B_HEADS_PER_GROUP * HEAD_DIM
SAMPLE_TILE_SEQS = 32

C_QA = (0, 512)
C_KVA = (512, 1024)
C_QKV1 = (1024, 1792)
C_GATES = (1792, 3840)
C_QKV2 = (3840, 4608)
C_QKV3 = (4608, 5376)
R_A = (0, 256)
R_B1 = (256, 768)
R_B2 = (768, 1280)
R_B3 = (1280, 1792)

VMEM_LIMIT = 56 * 1024 * 1024

_NT = (((1,), (1,)), ((), ()))


def _const_spec(shape):
    nd = len(shape)
    return pl.BlockSpec(shape, lambda *_: (0,) * nd, pipeline_mode=pl.Buffered(1))


def _rms(x, gain):
    ms = jnp.mean(x * x, axis=-1, keepdims=True)
    return x * lax.rsqrt(ms + RMS_EPS) * gain


def _alibi_slopes(n_heads):
    return jnp.exp2(-8.0 * jnp.arange(1, n_heads + 1, dtype=F32) / n_heads)


def _prep_in_weights(w):
    qa, ka, va = w[:, 0:512], w[:, 512:640], w[:, 640:768]
    qb, kb, vb = w[:, 768:1536], w[:, 1536:2304], w[:, 2304:3072]
    gates = w[:, 3072:5120]
    g = lambda z, i: z[:, GROUP_COLS * i:GROUP_COLS * (i + 1)]
    hd = HEAD_DIM
    kva2 = [z[:, i * hd:(i + 1) * hd] for z in (ka, va) for i in (0, 0, 1, 1)]
    w_rm = jnp.concatenate(
        [qa] + kva2 + [g(qb, 0), g(kb, 0), g(vb, 0), gates,
         g(qb, 1), g(kb, 1), g(vb, 1), g(qb, 2), g(kb, 2), g(vb, 2)], axis=1).astype(BF16)
    w_kvt = jnp.concatenate(
        [ka, va, g(kb, 0), g(vb, 0), g(kb, 1), g(vb, 1), g(kb, 2), g(vb, 2)], axis=1).T.astype(BF16)
    return w_rm, w_kvt


def _store_qkv(ref, r):
    ref[:, 0:GROUP_COLS] = (r[:, 0:GROUP_COLS] * Q_SCALE).astype(BF16)
    ref[:, GROUP_COLS:] = r[:, GROUP_COLS:].astype(BF16)


def _k1p_kernel(x_ref, g_ref, wrm_ref, wkvt_ref,
                qa_ref, kva_ref, qkv1_ref, gates_ref, qkv2_ref, qkv3_ref,
                at_ref, b1t_ref, b2t_ref, b3t_ref, slab_ref):
    tb = pl.program_id(1)
    gain = g_ref[...]

    def proj(hh, c):
        return jnp.dot(hh, wrm_ref[:, c[0]:c[1]], preferred_element_type=F32)

    h = _rms(x_ref[0], gain).astype(BF16)
    qa_ref[0] = (proj(h, C_QA) * Q_SCALE).astype(BF16)
    kva_ref[0] = proj(h, C_KVA).astype(BF16)
    _store_qkv(qkv1_ref.at[0], proj(h, C_QKV1))
    half = (C_GATES[0] + C_GATES[1]) // 2
    gates_ref[0, :, 0:D_MODEL] = proj(h, (C_GATES[0], half)).astype(BF16)
    gates_ref[0, :, D_MODEL:] = proj(h, (half, C_GATES[1])).astype(BF16)

    n_slabs = D_MODEL // LANES
    for s in range(n_slabs):
        slab_ref[s] = x_ref[0, :, s * LANES:(s + 1) * LANES]
    for dil, cols, out_ref in ((4, C_QKV2, qkv2_ref), (16, C_QKV3, qkv3_ref)):
        per = TOKEN_BLOCK // dil
        xs = jnp.concatenate(
            [jnp.concatenate([slab_ref[s, pl.ds(r, per, stride=dil), :] for r in range(dil)], axis=0)
             for s in range(n_slabs)], axis=1)
        hp = _rms(xs, gain).astype(BF16)
        rp = proj(hp, cols)
        out_ref[0, :, :, 0:GROUP_COLS] = (
            (rp[:, 0:GROUP_COLS] * Q_SCALE).astype(BF16).reshape(dil, per, GROUP_COLS))
        out_ref[0, :, :, GROUP_COLS:] = (
            rp[:, GROUP_COLS:].astype(BF16).reshape(dil, per, 2 * GROUP_COLS))

    b3t_ref[0] = lax.dot_general(wkvt_ref[R_B3[0]:R_B3[1], :], h, _NT, preferred_element_type=F32)

    @pl.when(tb == pl.num_programs(1) - 1)
    def _():
        b2t_ref[0] = lax.dot_general(wkvt_ref[R_B2[0]:R_B2[1], :], h, _NT, preferred_element_type=F32)
        t = lax.dot_general(wkvt_ref[R_A[0]:R_B1[1], :], h[TOKEN_BLOCK - BLOCK:, :], _NT,
                            preferred_element_type=F32)
        at_ref[0] = t[R_A[0]:R_A[1]]
        b1t_ref[0] = t[R_B1[0]:R_B1[1]]


def _in_proj_prompt(x, gain, w_rm, w_kvt):
    n, s, d = x.shape
    assert s % TOKEN_BLOCK == 0 and TOKEN_BLOCK == B_GROUPS[1][0] and s == B_GROUPS[2][0]
    nb = s // TOKEN_BLOCK
    out_shape = (
        jax.ShapeDtypeStruct((n, s, 512), BF16),
        jax.ShapeDtypeStruct((n, s, 512), BF16),
        jax.ShapeDtypeStruct((n, s, 768), BF16),
        jax.ShapeDtypeStruct((n, s, 2 * D_MODEL), BF16),
        jax.ShapeDtypeStruct((n, 4, s // 4, 768), BF16),
        jax.ShapeDtypeStruct((n, 16, s // 16, 768), BF16),
        jax.ShapeDtypeStruct((n, 256, BLOCK), F32),
        jax.ShapeDtypeStruct((n, 512, BLOCK), F32),
        jax.ShapeDtypeStruct((n, 512, TOKEN_BLOCK), F32),
        jax.ShapeDtypeStruct((n, 512, s), F32),
    )
    tok = lambda w: pl.BlockSpec((1, TOKEN_BLOCK, w), lambda i, j: (i, j, 0))
    out_specs = (
        tok(512), tok(512), tok(768), tok(2 * D_MODEL),
        pl.BlockSpec((1, 4, TOKEN_BLOCK // 4, 768), lambda i, j: (i, 0, j, 0)),
        pl.BlockSpec((1, 16, TOKEN_BLOCK // 16, 768), lambda i, j: (i, 0, j, 0)),
        pl.BlockSpec((1, 256, BLOCK), lambda i, j: (i, 0, 0)),
        pl.BlockSpec((1, 512, BLOCK), lambda i, j: (i, 0, 0)),
        pl.BlockSpec((1, 512, TOKEN_BLOCK), lambda i, j: (i, 0, 0)),
        pl.BlockSpec((1, 512, TOKEN_BLOCK), lambda i, j: (i, 0, j)),
    )
    return pl.pallas_call(
        _k1p_kernel,
        grid=(n, nb),
        in_specs=[tok(d), _const_spec((1, d)), _const_spec(w_rm.shape), _const_spec(w_kvt.shape)],
        out_specs=out_specs,
        out_shape=out_shape,
        scratch_shapes=[pltpu.VMEM((d // LANES, TOKEN_BLOCK, LANES), F32)],
        compiler_params=pltpu.CompilerParams(
            dimension_semantics=("arbitrary", "arbitrary"), vmem_limit_bytes=VMEM_LIMIT),
        name="in_proj_prompt",
    )(x, gain, w_rm, w_kvt)


def _prompt_bias_tables():
    i = jnp.arange(BLOCK)[:, None]
    j = jnp.arange(2 * BLOCK)[None, :]
    dist = (i + BLOCK) - j
    tabs = []
    slopes_a, slopes_b = _alibi_slopes(A_HEADS), _alibi_slopes(3 * B_HEADS_PER_GROUP)
    valid_a = (dist >= 0) & (dist <= A_WINDOW - 1)
    for h in range(A_HEADS):
        tabs.append(jnp.where(valid_a, -slopes_a[h] * dist.astype(F32), NEG_INF))
    for g, (win, dil) in enumerate(B_GROUPS):
        valid = (dist >= 0) & (dist <= win // dil)
        for hh in range(B_HEADS_PER_GROUP):
            s = slopes_b[g * B_HEADS_PER_GROUP + hh]
            tabs.append(jnp.where(valid, -s * (dist * dil).astype(F32), NEG_INF))
    return jnp.stack(tabs, axis=0).astype(F32)


def _band_pair(q_pair, kc, vc, kp, vp, biases, no_prev, sinks):
    low = lax.broadcasted_iota(jnp.int32, (BLOCK, 2 * HEAD_DIM), 1) < HEAD_DIM
    ones = jnp.ones((BLOCK, 2 * HEAD_DIM), BF16)
    res = []
    for keep, bias, sink in ((low, biases[0], sinks[0]), (~low, biases[1], sinks[1])):
        q = jnp.where(keep, q_pair, jnp.zeros_like(q_pair))
        sc = lax.dot_general(q, kc, _NT, preferred_element_type=F32) + bias[:, BLOCK:]
        if kp is not None:
            bp = jnp.where(no_prev, NEG_INF, bias[:, :BLOCK])
            sp = lax.dot_general(q, kp, _NT, preferred_element_type=F32) + bp
            m = jnp.max(jnp.maximum(sc, sp), axis=-1, keepdims=True)
        else:
            m = jnp.max(sc, axis=-1, keepdims=True)
        if sink is not None:
            m = jnp.maximum(m, sink)
        pc = jnp.exp(sc - m).astype(BF16)
        o = jnp.dot(pc, vc, preferred_element_type=F32)
        l = jnp.dot(pc, ones, preferred_element_type=F32)
        if kp is not None:
            pp = jnp.exp(sp - m).astype(BF16)
            o = o + jnp.dot(pp, vp, preferred_element_type=F32)
            l = l + jnp.dot(pp, ones, preferred_element_type=F32)
        if sink is not None:
            l = l + jnp.exp(sink - m)
        res.append((o / l, m + jnp.log(l)))
    return (jnp.where(low, res[0][0], res[1][0]), jnp.where(low, res[0][1], res[1][1]))


def _k2p_kernel(sink_ref, bias_ref,
                qa_ref, kva_ref, kvap_ref,
                q1_ref, q1p_ref, q2_ref, q2p_ref, q3_ref,
                oa_ref, og1_ref, og2_ref, og3_ref):
    j = pl.program_id(1)
    pw = 2 * HEAD_DIM
    first_blk = j == 0
    first_sub = (j % 4) == 0

    outs = []
    for p in range(A_HEADS // 2):
        g = (2 * p) // (A_HEADS // A_KV_HEADS)
        sl = lambda ref, part: ref[0, :, part * 2 * pw + g * pw:part * 2 * pw + (g + 1) * pw]
        o, _ = _band_pair(qa_ref[0, :, p * pw:(p + 1) * pw],
                          sl(kva_ref, 0), sl(kva_ref, 1), sl(kvap_ref, 0), sl(kvap_ref, 1),
                          (bias_ref[2 * p], bias_ref[2 * p + 1]), first_blk,
                          (sink_ref[0, 2 * p], sink_ref[0, 2 * p + 1]))
        outs.append(o)
    oa_ref[0] = jnp.concatenate(outs, axis=-1).astype(BF16)

    def group(cur, prev, base, no_prev, out_ref):
        os_, ls_ = [], []
        for p in range(B_HEADS_PER_GROUP // 2):
            sl = lambda ref, part: ref[:, part * GROUP_COLS + p * pw:part * GROUP_COLS + (p + 1) * pw]
            kp = vp = None
            if prev is not None:
                kp, vp = sl(prev, 1), sl(prev, 2)
            o, lse = _band_pair(sl(cur, 0), sl(cur, 1), sl(cur, 2), kp, vp,
                                (bias_ref[base + 2 * p], bias_ref[base + 2 * p + 1]), no_prev,
                                (None, None))
            os_.append(o)
            ls_.append(lse)
        out_ref[...] = jnp.concatenate(os_ + ls_, axis=-1)

    group(q1_ref.at[0], q1p_ref.at[0], A_HEADS, first_blk, og1_ref.at[0])
    group(q2_ref.at[0, 0], q2p_ref.at[0, 0], A_HEADS + 4, first_sub, og2_ref.at[0, 0])
    group(q3_ref.at[0, 0], None, A_HEADS + 8, None, og3_ref.at[0, 0])


def _attention_prompt(sinks, qa, kva, qkv1, qkv2, qkv3):
    n, s, _ = qa.shape
    nb = s // BLOCK
    assert nb == 16 and qkv2.shape[2] // BLOCK == 4 and qkv3.shape[2] == BLOCK
    bias = _prompt_bias_tables()
    prev = lambda j: jnp.maximum(j - 1, 0)
    in_specs = [
        pl.BlockSpec(memory_space=pltpu.SMEM),
        _const_spec(bias.shape),
        pl.BlockSpec((1, BLOCK, 512), lambda i, j: (i, j, 0)),
        pl.BlockSpec((1, BLOCK, 512), lambda i, j: (i, j, 0)),
        pl.BlockSpec((1, BLOCK, 512), lambda i, j: (i, prev(j), 0)),
        pl.BlockSpec((1, BLOCK, 768), lambda i, j: (i, j, 0)),
        pl.BlockSpec((1, BLOCK, 768), lambda i, j: (i, prev(j), 0)),
        pl.BlockSpec((1, 1, BLOCK, 768), lambda i, j: (i, j // 4, j % 4, 0)),
        pl.BlockSpec((1, 1, BLOCK, 768), lambda i, j: (i, j // 4, prev(j % 4), 0)),
        pl.BlockSpec((1, 1, BLOCK, 768), lambda i, j: (i, j, 0, 0)),
    ]
    out_shape = (
        jax.ShapeDtypeStruct((n, s, 512), BF16),
        jax.ShapeDtypeStruct((n, s, 512), F32),
        jax.ShapeDtypeStruct((n, 4, s // 4, 512), F32),
        jax.ShapeDtypeStruct((n, 16, s // 16, 512), F32),
    )
    out_specs = (
        pl.BlockSpec((1, BLOCK, 512), lambda i, j: (i, j, 0)),
        pl.BlockSpec((1, BLOCK, 512), lambda i, j: (i, j, 0)),
        pl.BlockSpec((1, 1, BLOCK, 512), lambda i, j: (i, j // 4, j % 4, 0)),
        pl.BlockSpec((1, 1, BLOCK, 512), lambda i, j: (i, j, 0, 0)),
    )
    return pl.pallas_call(
        _k2p_kernel,
        grid=(n, nb),
        in_specs=in_specs,
        out_specs=out_specs,
        out_shape=out_shape,
        compiler_params=pltpu.CompilerParams(
            dimension_semantics=("arbitrary", "arbitrary"), vmem_limit_bytes=VMEM_LIMIT),
        name="attention_prompt",
    )(sinks, bias, qa, kva, kva, qkv1, qkv1, qkv2, qkv2, qkv3)


def _sigmoid(x):
    return 1.0 / (1.0 + jnp.exp(-x))


def _mix_tail(oa, ob, gates_ref, x_ref, wba_ref, wbb_ref, wout_ref, gpost_ref, out_ref):
    ya = jnp.dot(oa, wba_ref[...], preferred_element_type=F32)
    yb = jnp.dot(ob, wbb_ref[...], preferred_element_type=F32)
    ga = gates_ref[:, 0:D_MODEL].astype(F32)
    gb = gates_ref[:, D_MODEL:].astype(F32)
    mixed = _sigmoid(ga) * ya + _sigmoid(gb) * yb
    mix = jnp.dot(mixed.astype(BF16), wout_ref[...], preferred_element_type=F32)
    out_ref[...] = x_ref[...] + _rms(mix, gpost_ref[...])


def _k3p_kernel(oa_ref, og1_ref, og2_ref, og3_ref, gates_ref, x_ref,
                wba_ref, wbb_ref, wout_ref, gpost_ref, out_ref, s2_ref, s3_ref):
    n_slabs = 2 * GROUP_COLS // LANES
    for dil, src, dst in ((4, og2_ref, s2_ref), (16, og3_ref, s3_ref)):
        for r in range(dil):
            for s in range(n_slabs):
                dst[s, pl.ds(r, TOKEN_BLOCK // dil, stride=dil), :] = (
                    src[0, r, :, s * LANES:(s + 1) * LANES])
    o1, l1 = og1_ref[0, :, 0:GROUP_COLS], og1_ref[0, :, GROUP_COLS:]
    o2 = jnp.concatenate([s2_ref[0], s2_ref[1]], axis=1)
    l2 = jnp.concatenate([s2_ref[2], s2_ref[3]], axis=1)
    o3 = jnp.concatenate([s3_ref[0], s3_ref[1]], axis=1)
    l3 = jnp.concatenate([s3_ref[2], s3_ref[3]], axis=1)
    m = jnp.maximum(jnp.maximum(l1, l2), l3)
    e1, e2, e3 = jnp.exp(l1 - m), jnp.exp(l2 - m), jnp.exp(l3 - m)
    ob = (e1 * o1 + e2 * o2 + e3 * o3) / (e1 + e2 + e3)
    _mix_tail(oa_ref[0], ob.astype(BF16), gates_ref.at[0], x_ref.at[0],
              wba_ref, wbb_ref, wout_ref, gpost_ref, out_ref.at[0])


def _out_proj_prompt(oa, og1, og2, og3, gates, x, wba, wbb, wout, gpost):
    n, s, d = x.shape
    nb = s // TOKEN_BLOCK
    tok = lambda w: pl.BlockSpec((1, TOKEN_BLOCK, w), lambda i, j: (i, j, 0))
    return pl.pallas_call(
        _k3p_kernel,
        grid=(n, nb),
        in_specs=[
            tok(512), tok(512),
            pl.BlockSpec((1, 4, TOKEN_BLOCK // 4, 512), lambda i, j: (i, 0, j, 0)),
            pl.BlockSpec((1, 16, TOKEN_BLOCK // 16, 512), lambda i, j: (i, 0, j, 0)),
            tok(2 * D_MODEL), tok(d),
            _const_spec(wba.shape), _const_spec(wbb.shape), _const_spec(wout.shape),
            _const_spec(gpost.shape),
        ],
        out_specs=tok(d),
        out_shape=jax.ShapeDtypeStruct((n, s, d), F32),
        scratch_shapes=[pltpu.VMEM((2 * GROUP_COLS // LANES, TOKEN_BLOCK, LANES), F32)] * 2,
        compiler_params=pltpu.CompilerParams(
            dimension_semantics=("arbitrary", "arbitrary"), vmem_limit_bytes=VMEM_LIMIT),
        name="out_proj_prompt",
    )(oa, og1, og2, og3, gates, x, wba, wbb, wout, gpost)


def _k3s_kernel(o_ref, gates_ref, x_ref, wba_ref, wbb_ref, wout_ref, gpost_ref, out_ref):
    oa = o_ref[:, 0:512].astype(BF16)
    ob = o_ref[:, 512:768].astype(BF16)
    _mix_tail(oa, ob, gates_ref, x_ref, wba_ref, wbb_ref, wout_ref, gpost_ref, out_ref)


def _out_proj_sample(o, gates, x, wba, wbb, wout, gpost):
    t, d = x.shape
    full = lambda a: pl.BlockSpec(a.shape, lambda i: (0,) * a.ndim)
    args = (o, gates, x, wba, wbb, wout, gpost)
    return pl.pallas_call(
        _k3s_kernel,
        grid=(1,),
        in_specs=[full(a) for a in args],
        out_specs=pl.BlockSpec((t, d), lambda i: (0, 0)),
        out_shape=jax.ShapeDtypeStruct((t, d), F32),
        compiler_params=pltpu.CompilerParams(
            dimension_semantics=("arbitrary",), vmem_limit_bytes=VMEM_LIMIT),
        name="out_proj_sample",
    )(*args)


FF_CHUNK = 512


def _gelu_tanh(c):
    return 0.5 * c * (1.0 + jnp.tanh(np.sqrt(2.0 / np.pi).astype(np.float32)
                                     * (c + 0.044715 * (c * c * c))))


def _ffn_chunk(h, w_a, w_g, cw, cb, w_down, prev_fn):
    a = jnp.dot(h, w_a, preferred_element_type=F32)
    g = jnp.dot(h, w_g, preferred_element_type=F32)
    p1, p2 = prev_fn(a)
    conv = cb + cw[0:1] * p2 + cw[1:2] * p1 + cw[2:3] * a
    act = (_gelu_tanh(conv) * g).astype(BF16)
    return jnp.dot(act, w_down, preferred_element_type=F32), a


def _k4p_kernel(x_ref, gpre_ref, gpost_ref, wup_ref, cw_ref, cb_ref, wdown_ref,
                out_ref, tail_ref):
    tb = pl.program_id(1)

    @pl.when(tb == 0)
    def _():
        tail_ref[...] = jnp.zeros_like(tail_ref)

    rows = lax.broadcasted_iota(jnp.int32, (TOKEN_BLOCK, FF_CHUNK), 0)
    x = x_ref[0]
    h = _rms(x, gpre_ref[...]).astype(BF16)
    y = jnp.zeros((TOKEN_BLOCK, D_MODEL), F32)
    for c0 in range(0, D_FF, FF_CHUNK):
        c1 = c0 + FF_CHUNK

        def prev(a, c0=c0, c1=c1):
            t6, t7 = tail_ref[0, 6:7, c0:c1], tail_ref[0, 7:8, c0:c1]
            p1 = jnp.where(rows == 0, t7, pltpu.roll(a, 1, axis=0))
            p2 = jnp.where(rows == 0, t6, jnp.where(rows == 1, t7, pltpu.roll(a, 2, axis=0)))
            return p1, p2

        yc, a = _ffn_chunk(h, wup_ref[:, c0:c1], wup_ref[:, D_FF + c0:D_FF + c1],
                           cw_ref[:, c0:c1], cb_ref[:, c0:c1], wdown_ref[c0:c1, :], prev)
        tail_ref[0, :, c0:c1] = a[TOKEN_BLOCK - 8:, :]
        y = y + yc
    out_ref[0] = x + _rms(y, gpost_ref[...])


def _ffn_prompt(x, gpre, gpost, wup, cw, cb, wdown):
    n, s, d = x.shape
    nb = s // TOKEN_BLOCK
    tok = pl.BlockSpec((1, TOKEN_BLOCK, d), lambda i, j: (i, j, 0))
    consts = (gpre, gpost, wup, cw, cb, wdown)
    return pl.pallas_call(
        _k4p_kernel,
        grid=(n, nb),
        in_specs=[tok] + [_const_spec(a.shape) for a in consts],
        out_specs=(tok, pl.BlockSpec((1, 8, D_FF), lambda i, j: (i, 0, 0))),
        out_shape=(jax.ShapeDtypeStruct((n, s, d), F32), jax.ShapeDtypeStruct((n, 8, D_FF), F32)),
        compiler_params=pltpu.CompilerParams(
            dimension_semantics=("arbitrary", "arbitrary"), vmem_limit_bytes=VMEM_LIMIT),
        name="ffn_prompt",
    )(x, *consts)


def _k4s_kernel(x_ref, e1_ref, e2_ref, gpre_ref, gpost_ref, wa_ref, wg_ref, cw_ref, cb_ref,
                wdown_ref, out_ref, a_ref, h_ref, y_ref):
    c = pl.program_id(0)

    @pl.when(c == 0)
    def _():
        h_ref[...] = _rms(x_ref[...], gpre_ref[...]).astype(BF16)
        y_ref[...] = jnp.zeros_like(y_ref)

    pos = lax.broadcasted_iota(jnp.int32, a_ref.shape, 0) % 4

    def prev(a):
        return (jnp.where(pos == 0, e1_ref[...], pltpu.roll(a, 1, axis=0)),
                jnp.where(pos < 2, e2_ref[...], pltpu.roll(a, 2, axis=0)))

    yc, a = _ffn_chunk(h_ref[...], wa_ref[...], wg_ref[...], cw_ref[...], cb_ref[...],
                       wdown_ref[...], prev)
    a_ref[...] = a
    y_ref[...] += yc

    @pl.when(c == pl.num_programs(0) - 1)
    def _():
        out_ref[...] = x_ref[...] + _rms(y_ref[...], gpost_ref[...])


def _ffn_sample(x, e1, e2, gpre, gpost, wup, cw, cb, wdown):
    t, d = x.shape
    nc = D_FF // FF_CHUNK
    const = lambda a: pl.BlockSpec(a.shape, lambda c: (0,) * a.ndim)
    cols = lambda rows: pl.BlockSpec((rows, FF_CHUNK), lambda c: (0, c))
    return pl.pallas_call(
        _k4s_kernel,
        grid=(nc,),
        in_specs=[const(x), cols(t), cols(t), const(gpre), const(gpost),
                  cols(d), pl.BlockSpec((d, FF_CHUNK), lambda c: (0, nc + c)),
                  cols(3), cols(1), pl.BlockSpec((FF_CHUNK, d), lambda c: (c, 0))],
        out_specs=(pl.BlockSpec((t, d), lambda c: (0, 0)), cols(t)),
        out_shape=(jax.ShapeDtypeStruct((t, d), F32), jax.ShapeDtypeStruct((t, D_FF), F32)),
        scratch_shapes=[pltpu.VMEM((t, d), BF16), pltpu.VMEM((t, d), F32)],
        compiler_params=pltpu.CompilerParams(
            dimension_semantics=("arbitrary",), vmem_limit_bytes=VMEM_LIMIT),
        name="ffn_sample",
    )(x, e1, e2, gpre, gpost, wup, wup, cw, cb, wdown)


def _k1s_kernel(x_ref, g_ref, wrm_ref, wkvt_ref, rep_ref, qbd_ref, kvt_ref, gates_ref):
    h = _rms(x_ref[...], g_ref[...]).astype(BF16)
    shape = (rep_ref.shape[0], GROUP_COLS)
    own = ((lax.broadcasted_iota(jnp.int32, shape, 0) % 16) // 4
           == lax.broadcasted_iota(jnp.int32, shape, 1) // HEAD_DIM)
    starts = (C_QA[0], C_QA[0] + GROUP_COLS, C_QKV1[0], C_QKV2[0], C_QKV3[0])
    for u, c0 in enumerate(starts):
        q = jnp.dot(h, wrm_ref[:, c0:c0 + GROUP_COLS], preferred_element_type=F32)
        q = (q * Q_SCALE).astype(BF16)
        qrep = jnp.dot(rep_ref[...], q, preferred_element_type=F32)
        qbd_ref[:, u * GROUP_COLS:(u + 1) * GROUP_COLS] = jnp.where(own, qrep, 0.0).astype(BF16)
    kvt_ref[...] = lax.dot_general(wkvt_ref[...], h, _NT, preferred_element_type=F32)
    half = (C_GATES[0] + C_GATES[1]) // 2
    gates_ref[:, 0:D_MODEL] = jnp.dot(h, wrm_ref[:, C_GATES[0]:half],
                                      preferred_element_type=F32).astype(BF16)
    gates_ref[:, D_MODEL:] = jnp.dot(h, wrm_ref[:, half:C_GATES[1]],
                                     preferred_element_type=F32).astype(BF16)


def _in_proj_sample(x, gain, w_rm, w_kvt):
    t, d = x.shape
    r = jnp.arange(4 * t)
    src = (r // 16) * 4 + r % 4
    rep = (src[:, None] == jnp.arange(t)[None, :]).astype(BF16)
    args = (x, gain, w_rm, w_kvt, rep)
    full = lambda a: pl.BlockSpec(a.shape, lambda i: (0,) * a.ndim, pipeline_mode=pl.Buffered(1))
    out_shape = (
        jax.ShapeDtypeStruct((4 * t, 1280), BF16),
        jax.ShapeDtypeStruct((w_kvt.shape[0], t), F32),
        jax.ShapeDtypeStruct((t, 2 * D_MODEL), BF16),
    )
    return pl.pallas_call(
        _k1s_kernel,
        grid=(1,),
        in_specs=[full(a) for a in args],
        out_specs=tuple(pl.BlockSpec(s.shape, lambda i: (0, 0)) for s in out_shape),
        out_shape=out_shape,
        compiler_params=pltpu.CompilerParams(
            dimension_semantics=("arbitrary",), vmem_limit_bytes=VMEM_LIMIT),
        name="in_proj_sample",
    )(*args)


_UNIT_COLS = ((0, 256), (256, 512), (512, 768), (768, 1024), (1024, 1280))
_UNIT_W = (A_WINDOW, A_WINDOW, B_GROUPS[0][0], B_GROUPS[1][0], B_GROUPS[2][0])
_UNIT_BIAS_OFF = tuple(int(v) for v in np.cumsum((0,) + _UNIT_W)[:-1])


def _sample_tables(sinks):
    slopes_a, slopes_b = _alibi_slopes(A_HEADS), _alibi_slopes(3 * B_HEADS_PER_GROUP)
    row = jnp.arange(16)
    slot, tok = row // 4, row % 4
    unit_slopes = [slopes_a[slot], slopes_a[4 + slot],
                   slopes_b[slot], slopes_b[4 + slot], slopes_b[8 + slot]]
    dils = (1, 1, 1, 4, 16)
    maxd = (A_WINDOW - 1, A_WINDOW - 1) + tuple(w for w, _ in B_GROUPS)
    bias = []
    for u in range(5):
        w = _UNIT_W[u]
        dist = (w + tok)[:, None] - jnp.arange(w)[None, :]
        valid = (dist % dils[u] == 0) & (dist <= maxd[u])
        bias.append(jnp.where(valid, -unit_slopes[u][:, None] * dist.astype(F32), NEG_INF))
    bias = jnp.concatenate(bias, axis=1).astype(F32)
    slope = jnp.broadcast_to(jnp.stack(unit_slopes)[:, :, None], (5, 16, 128)).astype(F32)
    sink = jnp.stack([sinks[slot], sinks[4 + slot]])
    sink = jnp.broadcast_to(sink[:, :, None], (2, 16, 128)).astype(F32)
    return bias, slope, sink


def _k2s_kernel(qbd_ref, new_ref, ca_ref, cb1_ref, cb2_ref, cb3_ref, bias_ref, slope_ref, sink_ref,
                na_ref, nb1_ref, nb2_ref, nb3_ref, o_ref):
    nl = pl.program_id(0) % SAMPLE_TILE_SEQS
    row = lax.broadcasted_iota(jnp.int32, (16, 128), 0)
    lane = lax.broadcasted_iota(jnp.int32, (16, 128), 1)
    tok, ltok = row % 4, lane % 4
    mine = (lane // 4) == nl
    dist_new = (tok - ltok).astype(F32)
    own = (lax.broadcasted_iota(jnp.int32, (16, GROUP_COLS), 0) // 4
           == lax.broadcasted_iota(jnp.int32, (16, GROUP_COLS), 1) // HEAD_DIM)
    pick = (lax.broadcasted_iota(jnp.int32, (8, 16), 1) % 4
            == lax.broadcasted_iota(jnp.int32, (8, 16), 0)).astype(BF16)

    def scores(u, kt, knew):
        q = qbd_ref[:, _UNIT_COLS[u][0]:_UNIT_COLS[u][1]]
        w = _UNIT_W[u]
        off = _UNIT_BIAS_OFF[u]
        s = jnp.dot(q, kt.astype(BF16), preferred_element_type=F32) + bias_ref[:, off:off + w]
        ok = mine & ((ltok <= tok) if u < 3 else (ltok == tok))
        bn = jnp.where(ok, -slope_ref[u] * dist_new, NEG_INF)
        sn = jnp.dot(q, knew.astype(BF16), preferred_element_type=F32) + bn
        m = jnp.maximum(jnp.max(s, axis=-1, keepdims=True), jnp.max(sn, axis=-1, keepdims=True))
        return s, sn, m

    def weighted(s, sn, m, vt, vnew):
        p, pn = jnp.exp(s - m), jnp.exp(sn - m)
        l = jnp.sum(p, axis=-1, keepdims=True) + jnp.sum(pn, axis=-1, keepdims=True)
        o = (lax.dot_general(p.astype(BF16), vt.astype(BF16), _NT, preferred_element_type=F32)
             + lax.dot_general(pn.astype(BF16), vnew.astype(BF16), _NT, preferred_element_type=F32))
        return o, l

    def finish(o, l):
        o = jnp.where(own, o / l, 0.0).astype(BF16)
        return jnp.dot(pick, o, preferred_element_type=F32)

    tile4 = lambda a: jnp.concatenate([a, a, a, a], axis=0)
    outs = []
    for g in range(A_KV_HEADS):
        kt = tile4(ca_ref[0, g * 64:(g + 1) * 64, :])
        vt = tile4(ca_ref[0, 128 + g * 64:128 + (g + 1) * 64, :])
        knew = tile4(new_ref[R_A[0] + g * 64:R_A[0] + (g + 1) * 64, :])
        vnew = tile4(new_ref[R_A[0] + 128 + g * 64:R_A[0] + 128 + (g + 1) * 64, :])
        s, sn, m = scores(g, kt, knew)
        sink = sink_ref[g][:, 0:1]
        m = jnp.maximum(m, sink)
        o, l = weighted(s, sn, m, vt, vnew)
        outs.append(finish(o, l + jnp.exp(sink - m)))

    parts = []
    for u, (c_ref, rows) in ((2, (cb1_ref, R_B1)), (3, (cb2_ref, R_B2)), (4, (cb3_ref, R_B3))):
        kt, vt = c_ref[0, 0:GROUP_COLS, :], c_ref[0, GROUP_COLS:, :]
        knew = new_ref[rows[0]:rows[0] + GROUP_COLS, :]
        vnew = new_ref[rows[0] + GROUP_COLS:rows[1], :]
        parts.append(scores(u, kt, knew) + (vt, vnew))
    m = functools.reduce(jnp.maximum, [p[2] for p in parts])
    o_sum, l_sum = 0.0, 0.0
    for s, sn, _, vt, vnew in parts:
        o, l = weighted(s, sn, m, vt, vnew)
        o_sum, l_sum = o_sum + o, l_sum + l
    outs.append(finish(o_sum, l_sum))
    o_ref[0] = jnp.concatenate(outs, axis=-1)

    lane_t = lax.broadcasted_iota(jnp.int32, (8, 128), 1)
    for c_ref, n_ref, rows in ((ca_ref, na_ref, R_A), (cb1_ref, nb1_ref, R_B1),
                               (cb2_ref, nb2_ref, R_B2), (cb3_ref, nb3_ref, R_B3)):
        w = c_ref.shape[2]
        nrows = rows[1] - rows[0]
        chunk = 128

        def body(i, carry, c_ref=c_ref, n_ref=n_ref, rows=rows, w=w):
            r0 = pl.multiple_of(i * chunk, chunk)
            x = c_ref[0, pl.ds(r0, chunk), :]
            shifted = pltpu.roll(x, w - 4, axis=1)
            new = new_ref[pl.ds(pl.multiple_of(rows[0] + r0, chunk), chunk), :]
            new = pltpu.roll(new, 124 - 4 * nl, axis=1)
            if w > 128:
                n_ref[0, pl.ds(r0, chunk), 0:w - 128] = shifted[:, 0:w - 128]
            sel = jnp.broadcast_to(lane_t[0:1] >= 124, (chunk, 128))
            n_ref[0, pl.ds(r0, chunk), w - 128:w] = jnp.where(sel, new, shifted[:, w - 128:w])
            return carry

        lax.fori_loop(0, nrows // chunk, body, 0)


def _attention_sample(qbd, kvt_new, ca, cb1, cb2, cb3, sinks):
    nseq = ca.shape[0]
    assert nseq % SAMPLE_TILE_SEQS == 0 and kvt_new.shape[1] == 4 * nseq
    assert (ca.shape[2], cb1.shape[2], cb2.shape[2], cb3.shape[2]) == _UNIT_W[1:]
    bias, slope, sink = _sample_tables(sinks)
    cache_spec = lambda a: pl.BlockSpec((1,) + a.shape[1:], lambda i: (i, 0, 0))
    in_specs = [
        pl.BlockSpec((16, 1280), lambda i: (i, 0)),
        pl.BlockSpec((kvt_new.shape[0], 128), lambda i: (0, i // SAMPLE_TILE_SEQS)),
        cache_spec(ca), cache_spec(cb1), cache_spec(cb2), cache_spec(cb3),
        _const_spec(bias.shape), _const_spec(slope.shape), _const_spec(sink.shape),
    ]
    out_shape = tuple(jax.ShapeDtypeStruct(a.shape, F32) for a in (ca, cb1, cb2, cb3)) + (
        jax.ShapeDtypeStruct((nseq, 8, 768), F32),)
    out_specs = tuple(cache_spec(a) for a in (ca, cb1, cb2, cb3)) + (
        pl.BlockSpec((1, 8, 768), lambda i: (i, 0, 0)),)
    return pl.pallas_call(
        _k2s_kernel,
        grid=(nseq,),
        in_specs=in_specs,
        out_specs=out_specs,
        out_shape=out_shape,
        compiler_params=pltpu.CompilerParams(
            dimension_semantics=("arbitrary",), vmem_limit_bytes=VMEM_LIMIT),
        name="attention_sample",
    )(qbd, kvt_new, ca, cb1, cb2, cb3, bias, slope, sink)


def _to_feature_major(cache):
    n, w = cache.shape[0], cache.shape[1]
    return jnp.transpose(cache, (0, 2, 3, 4, 1)).reshape(n, -1, w)


def _from_feature_major(c, heads):
    n, _, w = c.shape
    return jnp.transpose(c.reshape(n, 2, heads, HEAD_DIM, w), (0, 4, 1, 2, 3))[None]


def kernel(x_prompt, x_sample, cache_a_kv, cache_b1_kv, cache_b2_kv, cache_b3_kv, state_conv,
           w_in, sinks_a, w_branch_a, w_branch_b, w_out, norm_mix_pre, norm_mix_post,
           norm_ffn_pre, norm_ffn_post, w_up, conv_w, conv_b, w_down):
    assert w_in.shape[0] == 1, "single layer"
    w_rm, w_kvt = _prep_in_weights(w_in[0])
    wba, wbb, wout = (w_branch_a[0].astype(BF16), w_branch_b[0].astype(BF16), w_out[0].astype(BF16))
    wup, wdown = w_up[0].astype(BF16), w_down[0].astype(BF16)
    g_pre, g_post, gf_pre, gf_post = norm_mix_pre, norm_mix_post, norm_ffn_pre, norm_ffn_post
    cw, cb = conv_w[0], conv_b

    qa, kva, qkv1, gates, qkv2, qkv3, at, b1t, b2t, b3t = _in_proj_prompt(x_prompt, g_pre, w_rm, w_kvt)
    oa, og1, og2, og3 = _attention_prompt(sinks_a, qa, kva, qkv1, qkv2, qkv3)
    x1 = _out_proj_prompt(oa, og1, og2, og3, gates, x_prompt, wba, wbb, wout, g_post)
    y_prompt, tail = _ffn_prompt(x1, gf_pre, gf_post, wup, cw, cb, wdown)
    new_conv_p = tail[:, 6:8, :][None]

    nseq, tnew, d = x_sample.shape
    assert tnew == 4
    xs = x_sample.reshape(nseq * tnew, d)
    qbd, kvt_new, gates_s = _in_proj_sample(xs, g_pre, w_rm, w_kvt)
    ca, cb1, cb2, cb3 = (_to_feature_major(c[0]) for c in
                         (cache_a_kv, cache_b1_kv, cache_b2_kv, cache_b3_kv))
    na, nb1, nb2, nb3, o_s = _attention_sample(qbd, kvt_new, ca, cb1, cb2, cb3, sinks_a[0])
    o_s = o_s[:, 0:4, :].reshape(nseq * tnew, 768)
    x1s = _out_proj_sample(o_s, gates_s, xs, wba, wbb, wout, g_post)
    st = state_conv[0]
    zero = jnp.zeros_like(st[:, 0])
    e1 = jnp.stack([st[:, 1], zero, zero, zero], axis=1).reshape(nseq * tnew, D_FF)
    e2 = jnp.stack([st[:, 0], st[:, 1], zero, zero], axis=1).reshape(nseq * tnew, D_FF)
    ys, a_s = _ffn_sample(x1s, e1, e2, gf_pre, gf_post, wup, cw, cb, wdown)
    y_sample = ys.reshape(nseq, tnew, d)
    new_conv_s = a_s.reshape(nseq, tnew, D_FF)[:, 2:4, :][None]

    return (y_prompt, y_sample,
            _from_feature_major(at, A_KV_HEADS), _from_feature_major(na, A_KV_HEADS),
            _from_feature_major(b1t, 4), _from_feature_major(nb1, 4),
            _from_feature_major(b2t, 4), _from_feature_major(nb2, 4),
            _from_feature_major(b3t, 4), _from_feature_major(nb3, 4),
            new_conv_p, new_conv_s)
```

```python
import functools

import jax
import jax.numpy as jnp
import numpy as np
from jax import lax
from jax.experimental import pallas as pl
from jax.experimental.pallas import tpu as pltpu

F32 = jnp.float32
BF16 = jnp.bfloat16

HEAD_DIM = 64
A_HEADS = 8
A_KV_HEADS = 2
A_WINDOW = 128
B_GROUPS = ((128, 1), (512, 4), (2048, 16))
B_HEADS_PER_GROUP = 4
D_MODEL = 1024
D_FF = 4096
RMS_EPS = 1e-6
NEG_INF = -1e30
Q_SCALE = HEAD_DIM ** -0.5

LANES = 128
BLOCK = 128
TOKEN_BLOCK = 512
GROUP_COLS = B_HEADS_PER_GROUP * HEAD_DIM
SAMPLE_TILE_SEQS = 32

C_QA = (0, 512)
C_KVA = (512, 1024)
C_QKV1 = (1024, 1792)
C_GATES = (1792, 3840)
C_QKV2 = (3840, 4608)
C_QKV3 = (4608, 5376)
R_A = (0, 256)
R_B1 = (256, 768)
R_B2 = (768, 1280)
R_B3 = (1280, 1792)

VMEM_LIMIT = 56 * 1024 * 1024

_NT = (((1,), (1,)), ((), ()))


def _const_spec(shape):
    nd = len(shape)
    return pl.BlockSpec(shape, lambda *_: (0,) * nd, pipeline_mode=pl.Buffered(1))


def _rms(x, gain):
    ms = jnp.mean(x * x, axis=-1, keepdims=True)
    return x * lax.rsqrt(ms + RMS_EPS) * gain


def _alibi_slopes(n_heads):
    return jnp.exp2(-8.0 * jnp.arange(1, n_heads + 1, dtype=F32) / n_heads)


def _prep_in_weights(w):
    qa, ka, va = w[:, 0:512], w[:, 512:640], w[:, 640:768]
    qb, kb, vb = w[:, 768:1536], w[:, 1536:2304], w[:, 2304:3072]
    gates = w[:, 3072:5120]
    g = lambda z, i: z[:, GROUP_COLS * i:GROUP_COLS * (i + 1)]
    hd = HEAD_DIM
    kva2 = [z[:, i * hd:(i + 1) * hd] for z in (ka, va) for i in (0, 0, 1, 1)]
    w_rm = jnp.concatenate(
        [qa] + kva2 + [g(qb, 0), g(kb, 0), g(vb, 0), gates,
         g(qb, 1), g(kb, 1), g(vb, 1), g(qb, 2), g(kb, 2), g(vb, 2)], axis=1).astype(BF16)
    w_kvt = jnp.concatenate(
        [ka, va, g(kb, 0), g(vb, 0), g(kb, 1), g(vb, 1), g(kb, 2), g(vb, 2)], axis=1).T.astype(BF16)
    return w_rm, w_kvt


def _store_qkv(ref, r):
    ref[:, 0:GROUP_COLS] = (r[:, 0:GROUP_COLS] * Q_SCALE).astype(BF16)
    ref[:, GROUP_COLS:] = r[:, GROUP_COLS:].astype(BF16)


def _k1p_kernel(x_ref, g_ref, wrm_ref, wkvt_ref,
                qa_ref, kva_ref, qkv1_ref, gates_ref, qkv2_ref, qkv3_ref,
                at_ref, b1t_ref, b2t_ref, b3t_ref, slab_ref):
    tb = pl.program_id(1)
    gain = g_ref[...]

    def proj(hh, c):
        return jnp.dot(hh, wrm_ref[:, c[0]:c[1]], preferred_element_type=F32)

    h = _rms(x_ref[0], gain).astype(BF16)
    qa_ref[0] = (proj(h, C_QA) * Q_SCALE).astype(BF16)
    kva_ref[0] = proj(h, C_KVA).astype(BF16)
    _store_qkv(qkv1_ref.at[0], proj(h, C_QKV1))
    half = (C_GATES[0] + C_GATES[1]) // 2
    gates_ref[0, :, 0:D_MODEL] = proj(h, (C_GATES[0], half)).astype(BF16)
    gates_ref[0, :, D_MODEL:] = proj(h, (half, C_GATES[1])).astype(BF16)

    n_slabs = D_MODEL // LANES
    for s in range(n_slabs):
        slab_ref[s] = x_ref[0, :, s * LANES:(s + 1) * LANES]
    for dil, cols, out_ref in ((4, C_QKV2, qkv2_ref), (16, C_QKV3, qkv3_ref)):
        per = TOKEN_BLOCK // dil
        xs = jnp.concatenate(
            [jnp.concatenate([slab_ref[s, pl.ds(r, per, stride=dil), :] for r in range(dil)], axis=0)
             for s in range(n_slabs)], axis=1)
        hp = _rms(xs, gain).astype(BF16)
        rp = proj(hp, cols)
        out_ref[0, :, :, 0:GROUP_COLS] = (
            (rp[:, 0:GROUP_COLS] * Q_SCALE).astype(BF16).reshape(dil, per, GROUP_COLS))
        out_ref[0, :, :, GROUP_COLS:] = (
            rp[:, GROUP_COLS:].astype(BF16).reshape(dil, per, 2 * GROUP_COLS))

    b3t_ref[0] = lax.dot_general(wkvt_ref[R_B3[0]:R_B3[1], :], h, _NT, preferred_element_type=F32)

    @pl.when(tb == pl.num_programs(1) - 1)
    def _():
        b2t_ref[0] = lax.dot_general(wkvt_ref[R_B2[0]:R_B2[1], :], h, _NT, preferred_element_type=F32)
        t = lax.dot_general(wkvt_ref[R_A[0]:R_B1[1], :], h[TOKEN_BLOCK - BLOCK:, :], _NT,
                            preferred_element_type=F32)
        at_ref[0] = t[R_A[0]:R_A[1]]
        b1t_ref[0] = t[R_B1[0]:R_B1[1]]


def _in_proj_prompt(x, gain, w_rm, w_kvt):
    n, s, d = x.shape
    assert s % TOKEN_BLOCK == 0 and TOKEN_BLOCK == B_GROUPS[1][0] and s == B_GROUPS[2][0]
    nb = s // TOKEN_BLOCK
    out_shape = (
        jax.ShapeDtypeStruct((n, s, 512), BF16),
        jax.ShapeDtypeStruct((n, s, 512), BF16),
        jax.ShapeDtypeStruct((n, s, 768), BF16),
        jax.ShapeDtypeStruct((n, s, 2 * D_MODEL), BF16),
        jax.ShapeDtypeStruct((n, 4, s // 4, 768), BF16),
        jax.ShapeDtypeStruct((n, 16, s // 16, 768), BF16),
        jax.ShapeDtypeStruct((n, 256, BLOCK), F32),
        jax.ShapeDtypeStruct((n, 512, BLOCK), F32),
        jax.ShapeDtypeStruct((n, 512, TOKEN_BLOCK), F32),
        jax.ShapeDtypeStruct((n, 512, s), F32),
    )
    tok = lambda w: pl.BlockSpec((1, TOKEN_BLOCK, w), lambda i, j: (i, j, 0))
    out_specs = (
        tok(512), tok(512), tok(768), tok(2 * D_MODEL),
        pl.BlockSpec((1, 4, TOKEN_BLOCK // 4, 768), lambda i, j: (i, 0, j, 0)),
        pl.BlockSpec((1, 16, TOKEN_BLOCK // 16, 768), lambda i, j: (i, 0, j, 0)),
        pl.BlockSpec((1, 256, BLOCK), lambda i, j: (i, 0, 0)),
        pl.BlockSpec((1, 512, BLOCK), lambda i, j: (i, 0, 0)),
        pl.BlockSpec((1, 512, TOKEN_BLOCK), lambda i, j: (i, 0, 0)),
        pl.BlockSpec((1, 512, TOKEN_BLOCK), lambda i, j: (i, 0, j)),
    )
    return pl.pallas_call(
        _k1p_kernel,
        grid=(n, nb),
        in_specs=[tok(d), _const_spec((1, d)), _const_spec(w_rm.shape), _const_spec(w_kvt.shape)],
        out_specs=out_specs,
        out_shape=out_shape,
        scratch_shapes=[pltpu.VMEM((d // LANES, TOKEN_BLOCK, LANES), F32)],
        compiler_params=pltpu.CompilerParams(
            dimension_semantics=("arbitrary", "arbitrary"), vmem_limit_bytes=VMEM_LIMIT),
        name="in_proj_prompt",
    )(x, gain, w_rm, w_kvt)


def _prompt_bias_tables():
    i = jnp.arange(BLOCK)[:, None]
    j = jnp.arange(2 * BLOCK)[None, :]
    dist = (i + BLOCK) - j
    tabs = []
    slopes_a, slopes_b = _alibi_slopes(A_HEADS), _alibi_slopes(3 * B_HEADS_PER_GROUP)
    valid_a = (dist >= 0) & (dist <= A_WINDOW - 1)
    for h in range(A_HEADS):
        tabs.append(jnp.where(valid_a, -slopes_a[h] * dist.astype(F32), NEG_INF))
    for g, (win, dil) in enumerate(B_GROUPS):
        valid = (dist >= 0) & (dist <= win // dil)
        for hh in range(B_HEADS_PER_GROUP):
            s = slopes_b[g * B_HEADS_PER_GROUP + hh]
            tabs.append(jnp.where(valid, -s * (dist * dil).astype(F32), NEG_INF))
    return jnp.stack(tabs, axis=0).astype(F32)


def _band_unit(q_pairs, kcs, vcs, kps, vps, bias, no_prev, sink):
    low = lax.broadcasted_iota(jnp.int32, (BLOCK, 2 * HEAD_DIM), 1) < HEAD_DIM
    ones = jnp.ones((BLOCK, 2 * HEAD_DIM), BF16)
    q2s = []
    for qp in q_pairs:
        zero = jnp.zeros_like(qp)
        q2s.append(jnp.concatenate([jnp.where(low, qp, zero), jnp.where(low, zero, qp)], axis=0))
    nt = lambda a, b: lax.dot_general(a, b, _NT, preferred_element_type=F32)
    sc = jnp.concatenate([nt(q2, kc) for q2, kc in zip(q2s, kcs)], axis=0) + bias[:, BLOCK:]
    if kps is not None:
        bp = jnp.where(no_prev, NEG_INF, bias[:, :BLOCK])
        sp = jnp.concatenate([nt(q2, kp) for q2, kp in zip(q2s, kps)], axis=0) + bp
        m = jnp.max(jnp.maximum(sc, sp), axis=-1, keepdims=True)
    else:
        m = jnp.max(sc, axis=-1, keepdims=True)
    if sink is not None:
        m = jnp.maximum(m, sink[:, 0:1])
    pc = jnp.exp(sc - m).astype(BF16)
    l = jnp.dot(pc, ones, preferred_element_type=F32)
    o = [jnp.dot(pc[2 * BLOCK * p:2 * BLOCK * (p + 1)], vcs[p], preferred_element_type=F32)
         for p in range(2)]
    if kps is not None:
        pp = jnp.exp(sp - m).astype(BF16)
        l = l + jnp.dot(pp, ones, preferred_element_type=F32)
        o = [o[p] + jnp.dot(pp[2 * BLOCK * p:2 * BLOCK * (p + 1)], vps[p],
                            preferred_element_type=F32) for p in range(2)]
    if sink is not None:
        l = l + jnp.exp(sink - m)
    o = jnp.concatenate(o, axis=0) / l
    lse = m + jnp.log(l)
    pick = lambda a, p: jnp.where(low, a[2 * BLOCK * p:2 * BLOCK * p + BLOCK],
                                  a[2 * BLOCK * p + BLOCK:2 * BLOCK * (p + 1)])
    return [pick(o, 0), pick(o, 1)], [pick(lse, 0), pick(lse, 1)]


def _k2p_kernel(sink_ref, bias_ref,
                qa_ref, kva_ref, kvap_ref,
                q1_ref, q1p_ref, q2_ref, q2p_ref, q3_ref,
                oa_ref, og1_ref, og2_ref, og3_ref):
    j = pl.program_id(1)
    pw = 2 * HEAD_DIM
    first_blk = j == 0
    first_sub = (j % 4) == 0

    outs = []
    for g in range(A_KV_HEADS):
        sl = lambda ref, part: ref[0, :, part * 2 * pw + g * pw:part * 2 * pw + (g + 1) * pw]
        o, _ = _band_unit([qa_ref[0, :, (2 * g + p) * pw:(2 * g + p + 1) * pw] for p in range(2)],
                          [sl(kva_ref, 0)] * 2, [sl(kva_ref, 1)] * 2,
                          [sl(kvap_ref, 0)] * 2, [sl(kvap_ref, 1)] * 2,
                          bias_ref[g], first_blk, sink_ref[g])
        outs += o
    oa_ref[0] = jnp.concatenate(outs, axis=-1).astype(BF16)

    def group(cur, prev, unit, no_prev, out_ref):
        sl = lambda ref, part: [ref[:, part * GROUP_COLS + p * pw:part * GROUP_COLS + (p + 1) * pw]
                                for p in range(2)]
        kps = vps = None
        if prev is not None:
            kps, vps = sl(prev, 1), sl(prev, 2)
        o, lse = _band_unit(sl(cur, 0), sl(cur, 1), sl(cur, 2), kps, vps,
                            bias_ref[unit], no_prev, None)
        out_ref[...] = jnp.concatenate(o + lse, axis=-1)

    group(q1_ref.at[0], q1p_ref.at[0], 2, first_blk, og1_ref.at[0])
    group(q2_ref.at[0, 0], q2p_ref.at[0, 0], 3, first_sub, og2_ref.at[0, 0])
    group(q3_ref.at[0, 0], None, 4, None, og3_ref.at[0, 0])


def _attention_prompt(sinks, qa, kva, qkv1, qkv2, qkv3):
    n, s, _ = qa.shape
    nb = s // BLOCK
    assert nb == 16 and qkv2.shape[2] // BLOCK == 4 and qkv3.shape[2] == BLOCK
    bias = _prompt_bias_tables().reshape(5, 4 * BLOCK, 2 * BLOCK)
    sinks = jnp.broadcast_to(jnp.repeat(sinks.reshape(A_KV_HEADS, 4), BLOCK, axis=1)[:, :, None],
                             (A_KV_HEADS, 4 * BLOCK, 2 * HEAD_DIM)).astype(F32)
    prev = lambda j: jnp.maximum(j - 1, 0)
    in_specs = [
        _const_spec(sinks.shape),
        _const_spec(bias.shape),
        pl.BlockSpec((1, BLOCK, 512), lambda i, j: (i, j, 0)),
        pl.BlockSpec((1, BLOCK, 512), lambda i, j: (i, j, 0)),
        pl.BlockSpec((1, BLOCK, 512), lambda i, j: (i, prev(j), 0)),
        pl.BlockSpec((1, BLOCK, 768), lambda i, j: (i, j, 0)),
        pl.BlockSpec((1, BLOCK, 768), lambda i, j: (i, prev(j), 0)),
        pl.BlockSpec((1, 1, BLOCK, 768), lambda i, j: (i, j // 4, j % 4, 0)),
        pl.BlockSpec((1, 1, BLOCK, 768), lambda i, j: (i, j // 4, prev(j % 4), 0)),
        pl.BlockSpec((1, 1, BLOCK, 768), lambda i, j: (i, j, 0, 0)),
    ]
    out_shape = (
        jax.ShapeDtypeStruct((n, s, 512), BF16),
        jax.ShapeDtypeStruct((n, s, 512), F32),
        jax.ShapeDtypeStruct((n, 4, s // 4, 512), F32),
        jax.ShapeDtypeStruct((n, 16, s // 16, 512), F32),
    )
    out_specs = (
        pl.BlockSpec((1, BLOCK, 512), lambda i, j: (i, j, 0)),
        pl.BlockSpec((1, BLOCK, 512), lambda i, j: (i, j, 0)),
        pl.BlockSpec((1, 1, BLOCK, 512), lambda i, j: (i, j // 4, j % 4, 0)),
        pl.BlockSpec((1, 1, BLOCK, 512), lambda i, j: (i, j, 0, 0)),
    )
    return pl.pallas_call(
        _k2p_kernel,
        grid=(n, nb),
        in_specs=in_specs,
        out_specs=out_specs,
        out_shape=out_shape,
        compiler_params=pltpu.CompilerParams(
            dimension_semantics=("arbitrary", "arbitrary"), vmem_limit_bytes=VMEM_LIMIT),
        name="attention_prompt",
    )(sinks, bias, qa, kva, kva, qkv1, qkv1, qkv2, qkv2, qkv3)


def _sigmoid(x):
    return 1.0 / (1.0 + jnp.exp(-x))


def _mix_tail(oa, ob, gates_ref, x_ref, wba_ref, wbb_ref, wout_ref, gpost_ref, out_ref):
    ya = jnp.dot(oa, wba_ref[...], preferred_element_type=F32)
    yb = jnp.dot(ob, wbb_ref[...], preferred_element_type=F32)
    ga = gates_ref[:, 0:D_MODEL].astype(F32)
    gb = gates_ref[:, D_MODEL:].astype(F32)
    mixed = _sigmoid(ga) * ya + _sigmoid(gb) * yb
    mix = jnp.dot(mixed.astype(BF16), wout_ref[...], preferred_element_type=F32)
    out_ref[...] = x_ref[...] + _rms(mix, gpost_ref[...])


def _k3p_kernel(oa_ref, og1_ref, og2_ref, og3_ref, gates_ref, x_ref,
                wba_ref, wbb_ref, wout_ref, gpost_ref, out_ref, s2_ref, s3_ref):
    n_slabs = 2 * GROUP_COLS // LANES
    for dil, src, dst in ((4, og2_ref, s2_ref), (16, og3_ref, s3_ref)):
        for r in range(dil):
            for s in range(n_slabs):
                dst[s, pl.ds(r, TOKEN_BLOCK // dil, stride=dil), :] = (
                    src[0, r, :, s * LANES:(s + 1) * LANES])
    o1, l1 = og1_ref[0, :, 0:GROUP_COLS], og1_ref[0, :, GROUP_COLS:]
    o2 = jnp.concatenate([s2_ref[0], s2_ref[1]], axis=1)
    l2 = jnp.concatenate([s2_ref[2], s2_ref[3]], axis=1)
    o3 = jnp.concatenate([s3_ref[0], s3_ref[1]], axis=1)
    l3 = jnp.concatenate([s3_ref[2], s3_ref[3]], axis=1)
    m = jnp.maximum(jnp.maximum(l1, l2), l3)
    e1, e2, e3 = jnp.exp(l1 - m), jnp.exp(l2 - m), jnp.exp(l3 - m)
    ob = (e1 * o1 + e2 * o2 + e3 * o3) / (e1 + e2 + e3)
    _mix_tail(oa_ref[0], ob.astype(BF16), gates_ref.at[0], x_ref.at[0],
              wba_ref, wbb_ref, wout_ref, gpost_ref, out_ref.at[0])


def _out_proj_prompt(oa, og1, og2, og3, gates, x, wba, wbb, wout, gpost):
    n, s, d = x.shape
    nb = s // TOKEN_BLOCK
    tok = lambda w: pl.BlockSpec((1, TOKEN_BLOCK, w), lambda i, j: (i, j, 0))
    return pl.pallas_call(
        _k3p_kernel,
        grid=(n, nb),
        in_specs=[
            tok(512), tok(512),
            pl.BlockSpec((1, 4, TOKEN_BLOCK // 4, 512), lambda i, j: (i, 0, j, 0)),
            pl.BlockSpec((1, 16, TOKEN_BLOCK // 16, 512), lambda i, j: (i, 0, j, 0)),
            tok(2 * D_MODEL), tok(d),
            _const_spec(wba.shape), _const_spec(wbb.shape), _const_spec(wout.shape),
            _const_spec(gpost.shape),
        ],
        out_specs=tok(d),
        out_shape=jax.ShapeDtypeStruct((n, s, d), F32),
        scratch_shapes=[pltpu.VMEM((2 * GROUP_COLS // LANES, TOKEN_BLOCK, LANES), F32)] * 2,
        compiler_params=pltpu.CompilerParams(
            dimension_semantics=("arbitrary", "arbitrary"), vmem_limit_bytes=VMEM_LIMIT),
        name="out_proj_prompt",
    )(oa, og1, og2, og3, gates, x, wba, wbb, wout, gpost)


def _k3s_kernel(o_ref, gates_ref, x_ref, wba_ref, wbb_ref, wout_ref, gpost_ref, out_ref):
    oa = o_ref[:, 0:512].astype(BF16)
    ob = o_ref[:, 512:768].astype(BF16)
    _mix_tail(oa, ob, gates_ref, x_ref, wba_ref, wbb_ref, wout_ref, gpost_ref, out_ref)


def _out_proj_sample(o, gates, x, wba, wbb, wout, gpost):
    t, d = x.shape
    full = lambda a: pl.BlockSpec(a.shape, lambda i: (0,) * a.ndim)
    args = (o, gates, x, wba, wbb, wout, gpost)
    return pl.pallas_call(
        _k3s_kernel,
        grid=(1,),
        in_specs=[full(a) for a in args],
        out_specs=pl.BlockSpec((t, d), lambda i: (0, 0)),
        out_shape=jax.ShapeDtypeStruct((t, d), F32),
        compiler_params=pltpu.CompilerParams(
            dimension_semantics=("arbitrary",), vmem_limit_bytes=VMEM_LIMIT),
        name="out_proj_sample",
    )(*args)


FF_CHUNK = 512


def _gelu_tanh(c):
    return 0.5 * c * (1.0 + jnp.tanh(np.sqrt(2.0 / np.pi).astype(np.float32)
                                     * (c + 0.044715 * (c * c * c))))


def _ffn_chunk(h, w_a, w_g, cw, cb, w_down, prev_fn):
    a = jnp.dot(h, w_a, preferred_element_type=F32)
    g = jnp.dot(h, w_g, preferred_element_type=F32)
    p1, p2 = prev_fn(a)
    conv = cb + cw[0:1] * p2 + cw[1:2] * p1 + cw[2:3] * a
    act = (_gelu_tanh(conv) * g).astype(BF16)
    return jnp.dot(act, w_down, preferred_element_type=F32), a


def _k4p_kernel(x_ref, gpre_ref, gpost_ref, wup_ref, cw_ref, cb_ref, wdown_ref,
                out_ref, tail_ref):
    tb = pl.program_id(1)

    @pl.when(tb == 0)
    def _():
        tail_ref[...] = jnp.zeros_like(tail_ref)

    rows = lax.broadcasted_iota(jnp.int32, (TOKEN_BLOCK, FF_CHUNK), 0)
    x = x_ref[0]
    h = _rms(x, gpre_ref[...]).astype(BF16)
    y = jnp.zeros((TOKEN_BLOCK, D_MODEL), F32)
    for c0 in range(0, D_FF, FF_CHUNK):
        c1 = c0 + FF_CHUNK

        def prev(a, c0=c0, c1=c1):
            t6, t7 = tail_ref[0, 6:7, c0:c1], tail_ref[0, 7:8, c0:c1]
            p1 = jnp.where(rows == 0, t7, pltpu.roll(a, 1, axis=0))
            p2 = jnp.where(rows == 0, t6, jnp.where(rows == 1, t7, pltpu.roll(a, 2, axis=0)))
            return p1, p2

        yc, a = _ffn_chunk(h, wup_ref[:, c0:c1], wup_ref[:, D_FF + c0:D_FF + c1],
                           cw_ref[:, c0:c1], cb_ref[:, c0:c1], wdown_ref[c0:c1, :], prev)
        tail_ref[0, :, c0:c1] = a[TOKEN_BLOCK - 8:, :]
        y = y + yc
    out_ref[0] = x + _rms(y, gpost_ref[...])


def _ffn_prompt(x, gpre, gpost, wup, cw, cb, wdown):
    n, s, d = x.shape
    nb = s // TOKEN_BLOCK
    tok = pl.BlockSpec((1, TOKEN_BLOCK, d), lambda i, j: (i, j, 0))
    consts = (gpre, gpost, wup, cw, cb, wdown)
    return pl.pallas_call(
        _k4p_kernel,
        grid=(n, nb),
        in_specs=[tok] + [_const_spec(a.shape) for a in consts],
        out_specs=(tok, pl.BlockSpec((1, 8, D_FF), lambda i, j: (i, 0, 0))),
        out_shape=(jax.ShapeDtypeStruct((n, s, d), F32), jax.ShapeDtypeStruct((n, 8, D_FF), F32)),
        compiler_params=pltpu.CompilerParams(
            dimension_semantics=("arbitrary", "arbitrary"), vmem_limit_bytes=VMEM_LIMIT),
        name="ffn_prompt",
    )(x, *consts)


def _k4s_kernel(x_ref, e1_ref, e2_ref, gpre_ref, gpost_ref, wa_ref, wg_ref, cw_ref, cb_ref,
                wdown_ref, out_ref, a_ref, h_ref, y_ref):
    c = pl.program_id(0)

    @pl.when(c == 0)
    def _():
        h_ref[...] = _rms(x_ref[...], gpre_ref[...]).astype(BF16)
        y_ref[...] = jnp.zeros_like(y_ref)

    pos = lax.broadcasted_iota(jnp.int32, a_ref.shape, 0) % 4

    def prev(a):
        return (jnp.where(pos == 0, e1_ref[...], pltpu.roll(a, 1, axis=0)),
                jnp.where(pos < 2, e2_ref[...], pltpu.roll(a, 2, axis=0)))

    yc, a = _ffn_chunk(h_ref[...], wa_ref[...], wg_ref[...], cw_ref[...], cb_ref[...],
                       wdown_ref[...], prev)
    a_ref[...] = a
    y_ref[...] += yc

    @pl.when(c == pl.num_programs(0) - 1)
    def _():
        out_ref[...] = x_ref[...] + _rms(y_ref[...], gpost_ref[...])


def _ffn_sample(x, e1, e2, gpre, gpost, wup, cw, cb, wdown):
    t, d = x.shape
    nc = D_FF // FF_CHUNK
    const = lambda a: pl.BlockSpec(a.shape, lambda c: (0,) * a.ndim)
    cols = lambda rows: pl.BlockSpec((rows, FF_CHUNK), lambda c: (0, c))
    return pl.pallas_call(
        _k4s_kernel,
        grid=(nc,),
        in_specs=[const(x), cols(t), cols(t), const(gpre), const(gpost),
                  cols(d), pl.BlockSpec((d, FF_CHUNK), lambda c: (0, nc + c)),
                  cols(3), cols(1), pl.BlockSpec((FF_CHUNK, d), lambda c: (c, 0))],
        out_specs=(pl.BlockSpec((t, d), lambda c: (0, 0)), cols(t)),
        out_shape=(jax.ShapeDtypeStruct((t, d), F32), jax.ShapeDtypeStruct((t, D_FF), F32)),
        scratch_shapes=[pltpu.VMEM((t, d), BF16), pltpu.VMEM((t, d), F32)],
        compiler_params=pltpu.CompilerParams(
            dimension_semantics=("arbitrary",), vmem_limit_bytes=VMEM_LIMIT),
        name="ffn_sample",
    )(x, e1, e2, gpre, gpost, wup, wup, cw, cb, wdown)


def _k1s_kernel(x_ref, g_ref, wrm_ref, wkvt_ref, rep_ref, qbd_ref, kvt_ref, gates_ref):
    h = _rms(x_ref[...], g_ref[...]).astype(BF16)
    shape = (rep_ref.shape[0], GROUP_COLS)
    own = ((lax.broadcasted_iota(jnp.int32, shape, 0) % 16) // 4
           == lax.broadcasted_iota(jnp.int32, shape, 1) // HEAD_DIM)
    starts = (C_QA[0], C_QA[0] + GROUP_COLS, C_QKV1[0], C_QKV2[0], C_QKV3[0])
    for u, c0 in enumerate(starts):
        q = jnp.dot(h, wrm_ref[:, c0:c0 + GROUP_COLS], preferred_element_type=F32)
        q = (q * Q_SCALE).astype(BF16)
        qrep = jnp.dot(rep_ref[...], q, preferred_element_type=F32)
        qbd_ref[:, u * GROUP_COLS:(u + 1) * GROUP_COLS] = jnp.where(own, qrep, 0.0).astype(BF16)
    kvt_ref[...] = lax.dot_general(wkvt_ref[...], h, _NT, preferred_element_type=F32)
    half = (C_GATES[0] + C_GATES[1]) // 2
    gates_ref[:, 0:D_MODEL] = jnp.dot(h, wrm_ref[:, C_GATES[0]:half],
                                      preferred_element_type=F32).astype(BF16)
    gates_ref[:, D_MODEL:] = jnp.dot(h, wrm_ref[:, half:C_GATES[1]],
                                     preferred_element_type=F32).astype(BF16)


def _in_proj_sample(x, gain, w_rm, w_kvt):
    t, d = x.shape
    r = jnp.arange(4 * t)
    src = (r // 16) * 4 + r % 4
    rep = (src[:, None] == jnp.arange(t)[None, :]).astype(BF16)
    args = (x, gain, w_rm, w_kvt, rep)
    full = lambda a: pl.BlockSpec(a.shape, lambda i: (0,) * a.ndim, pipeline_mode=pl.Buffered(1))
    out_shape = (
        jax.ShapeDtypeStruct((4 * t, 1280), BF16),
        jax.ShapeDtypeStruct((w_kvt.shape[0], t), F32),
        jax.ShapeDtypeStruct((t, 2 * D_MODEL), BF16),
    )
    return pl.pallas_call(
        _k1s_kernel,
        grid=(1,),
        in_specs=[full(a) for a in args],
        out_specs=tuple(pl.BlockSpec(s.shape, lambda i: (0, 0)) for s in out_shape),
        out_shape=out_shape,
        compiler_params=pltpu.CompilerParams(
            dimension_semantics=("arbitrary",), vmem_limit_bytes=VMEM_LIMIT),
        name="in_proj_sample",
    )(*args)


_UNIT_COLS = ((0, 256), (256, 512), (512, 768), (768, 1024), (1024, 1280))
_UNIT_W = (A_WINDOW, A_WINDOW, B_GROUPS[0][0], B_GROUPS[1][0], B_GROUPS[2][0])
_UNIT_BIAS_OFF = tuple(int(v) for v in np.cumsum((0,) + _UNIT_W)[:-1])


def _sample_tables(sinks):
    slopes_a, slopes_b = _alibi_slopes(A_HEADS), _alibi_slopes(3 * B_HEADS_PER_GROUP)
    row = jnp.arange(16)
    slot, tok = row // 4, row % 4
    unit_slopes = [slopes_a[slot], slopes_a[4 + slot],
                   slopes_b[slot], slopes_b[4 + slot], slopes_b[8 + slot]]
    dils = (1, 1, 1, 4, 16)
    maxd = (A_WINDOW - 1, A_WINDOW - 1) + tuple(w for w, _ in B_GROUPS)
    bias = []
    for u in range(5):
        w = _UNIT_W[u]
        dist = (w + tok)[:, None] - jnp.arange(w)[None, :]
        valid = (dist % dils[u] == 0) & (dist <= maxd[u])
        bias.append(jnp.where(valid, -unit_slopes[u][:, None] * dist.astype(F32), NEG_INF))
    bias = jnp.concatenate(bias, axis=1).astype(F32)
    slope = jnp.broadcast_to(jnp.stack(unit_slopes)[:, :, None], (5, 16, 128)).astype(F32)
    sink = jnp.stack([sinks[slot], sinks[4 + slot]])
    sink = jnp.broadcast_to(sink[:, :, None], (2, 16, 128)).astype(F32)
    return bias, slope, sink


def _k2s_kernel(qbd_ref, new_ref, ca_ref, cb1_ref, cb2_ref, cb3_ref, bias_ref, slope_ref, sink_ref,
                na_ref, nb1_ref, nb2_ref, nb3_ref, o_ref):
    nl = pl.program_id(0) % SAMPLE_TILE_SEQS
    row = lax.broadcasted_iota(jnp.int32, (16, 128), 0)
    lane = lax.broadcasted_iota(jnp.int32, (16, 128), 1)
    tok, ltok = row % 4, lane % 4
    mine = (lane // 4) == nl
    dist_new = (tok - ltok).astype(F32)
    own = (lax.broadcasted_iota(jnp.int32, (16, GROUP_COLS), 0) // 4
           == lax.broadcasted_iota(jnp.int32, (16, GROUP_COLS), 1) // HEAD_DIM)
    pick = (lax.broadcasted_iota(jnp.int32, (8, 16), 1) % 4
            == lax.broadcasted_iota(jnp.int32, (8, 16), 0)).astype(BF16)

    def scores(u, kt, knew):
        q = qbd_ref[:, _UNIT_COLS[u][0]:_UNIT_COLS[u][1]]
        w = _UNIT_W[u]
        off = _UNIT_BIAS_OFF[u]
        s = jnp.dot(q, kt.astype(BF16), preferred_element_type=F32) + bias_ref[:, off:off + w]
        ok = mine & ((ltok <= tok) if u < 3 else (ltok == tok))
        bn = jnp.where(ok, -slope_ref[u] * dist_new, NEG_INF)
        sn = jnp.dot(q, knew.astype(BF16), preferred_element_type=F32) + bn
        m = jnp.maximum(jnp.max(s, axis=-1, keepdims=True), jnp.max(sn, axis=-1, keepdims=True))
        return s, sn, m

    def weighted(s, sn, m, vt, vnew):
        p, pn = jnp.exp(s - m), jnp.exp(sn - m)
        l = jnp.sum(p, axis=-1, keepdims=True) + jnp.sum(pn, axis=-1, keepdims=True)
        o = (lax.dot_general(p.astype(BF16), vt.astype(BF16), _NT, preferred_element_type=F32)
             + lax.dot_general(pn.astype(BF16), vnew.astype(BF16), _NT, preferred_element_type=F32))
        return o, l

    def finish(o, l):
        o = jnp.where(own, o / l, 0.0).astype(BF16)
        return jnp.dot(pick, o, preferred_element_type=F32)

    tile4 = lambda a: jnp.concatenate([a, a, a, a], axis=0)
    outs = []
    for g in range(A_KV_HEADS):
        kt = tile4(ca_ref[0, g * 64:(g + 1) * 64, :])
        vt = tile4(ca_ref[0, 128 + g * 64:128 + (g + 1) * 64, :])
        knew = tile4(new_ref[R_A[0] + g * 64:R_A[0] + (g + 1) * 64, :])
        vnew = tile4(new_ref[R_A[0] + 128 + g * 64:R_A[0] + 128 + (g + 1) * 64, :])
        s, sn, m = scores(g, kt, knew)
        sink = sink_ref[g][:, 0:1]
        m = jnp.maximum(m, sink)
        o, l = weighted(s, sn, m, vt, vnew)
        outs.append(finish(o, l + jnp.exp(sink - m)))

    parts = []
    for u, (c_ref, rows) in ((2, (cb1_ref, R_B1)), (3, (cb2_ref, R_B2)), (4, (cb3_ref, R_B3))):
        kt, vt = c_ref[0, 0:GROUP_COLS, :], c_ref[0, GROUP_COLS:, :]
        knew = new_ref[rows[0]:rows[0] + GROUP_COLS, :]
        vnew = new_ref[rows[0] + GROUP_COLS:rows[1], :]
        parts.append(scores(u, kt, knew) + (vt, vnew))
    m = functools.reduce(jnp.maximum, [p[2] for p in parts])
    o_sum, l_sum = 0.0, 0.0
    for s, sn, _, vt, vnew in parts:
        o, l = weighted(s, sn, m, vt, vnew)
        o_sum, l_sum = o_sum + o, l_sum + l
    outs.append(finish(o_sum, l_sum))
    o_ref[0] = jnp.concatenate(outs, axis=-1)

    lane_t = lax.broadcasted_iota(jnp.int32, (8, 128), 1)
    for c_ref, n_ref, rows in ((ca_ref, na_ref, R_A), (cb1_ref, nb1_ref, R_B1),
                               (cb2_ref, nb2_ref, R_B2), (cb3_ref, nb3_ref, R_B3)):
        w = c_ref.shape[2]
        nrows = rows[1] - rows[0]
        chunk = 256

        def body(i, carry, c_ref=c_ref, n_ref=n_ref, rows=rows, w=w):
            r0 = pl.multiple_of(i * chunk, chunk)
            x = c_ref[0, pl.ds(r0, chunk), :]
            shifted = pltpu.roll(x, w - 4, axis=1)
            new = new_ref[pl.ds(pl.multiple_of(rows[0] + r0, chunk), chunk), :]
            new = pltpu.roll(new, 124 - 4 * nl, axis=1)
            if w > 128:
                n_ref[0, pl.ds(r0, chunk), 0:w - 128] = shifted[:, 0:w - 128]
            sel = jnp.broadcast_to(lane_t[0:1] >= 124, (chunk, 128))
            n_ref[0, pl.ds(r0, chunk), w - 128:w] = jnp.where(sel, new, shifted[:, w - 128:w])
            return carry

        lax.fori_loop(0, nrows // chunk, body, 0)


def _attention_sample(qbd, kvt_new, ca, cb1, cb2, cb3, sinks):
    nseq = ca.shape[0]
    assert nseq % SAMPLE_TILE_SEQS == 0 and kvt_new.shape[1] == 4 * nseq
    assert (ca.shape[2], cb1.shape[2], cb2.shape[2], cb3.shape[2]) == _UNIT_W[1:]
    bias, slope, sink = _sample_tables(sinks)
    cache_spec = lambda a: pl.BlockSpec((1,) + a.shape[1:], lambda i: (i, 0, 0))
    in_specs = [
        pl.BlockSpec((16, 1280), lambda i: (i, 0)),
        pl.BlockSpec((kvt_new.shape[0], 128), lambda i: (0, i // SAMPLE_TILE_SEQS)),
        cache_spec(ca), cache_spec(cb1), cache_spec(cb2), cache_spec(cb3),
        _const_spec(bias.shape), _const_spec(slope.shape), _const_spec(sink.shape),
    ]
    out_shape = tuple(jax.ShapeDtypeStruct(a.shape, F32) for a in (ca, cb1, cb2, cb3)) + (
        jax.ShapeDtypeStruct((nseq, 8, 768), F32),)
    out_specs = tuple(cache_spec(a) for a in (ca, cb1, cb2, cb3)) + (
        pl.BlockSpec((1, 8, 768), lambda i: (i, 0, 0)),)
    return pl.pallas_call(
        _k2s_kernel,
        grid=(nseq,),
        in_specs=in_specs,
        out_specs=out_specs,
        out_shape=out_shape,
        compiler_params=pltpu.CompilerParams(
            dimension_semantics=("arbitrary",), vmem_limit_bytes=VMEM_LIMIT),
        name="attention_sample",
    )(qbd, kvt_new, ca, cb1, cb2, cb3, bias, slope, sink)


def _to_feature_major(cache):
    n, w = cache.shape[0], cache.shape[1]
    return jnp.transpose(cache, (0, 2, 3, 4, 1)).reshape(n, -1, w)


def _from_feature_major(c, heads):
    n, _, w = c.shape
    return jnp.transpose(c.reshape(n, 2, heads, HEAD_DIM, w), (0, 4, 1, 2, 3))[None]


def kernel(x_prompt, x_sample, cache_a_kv, cache_b1_kv, cache_b2_kv, cache_b3_kv, state_conv,
           w_in, sinks_a, w_branch_a, w_branch_b, w_out, norm_mix_pre, norm_mix_post,
           norm_ffn_pre, norm_ffn_post, w_up, conv_w, conv_b, w_down):
    assert w_in.shape[0] == 1, "single layer"
    w_rm, w_kvt = _prep_in_weights(w_in[0])
    wba, wbb, wout = (w_branch_a[0].astype(BF16), w_branch_b[0].astype(BF16), w_out[0].astype(BF16))
    wup, wdown = w_up[0].astype(BF16), w_down[0].astype(BF16)
    g_pre, g_post, gf_pre, gf_post = norm_mix_pre, norm_mix_post, norm_ffn_pre, norm_ffn_post
    cw, cb = conv_w[0], conv_b

    qa, kva, qkv1, gates, qkv2, qkv3, at, b1t, b2t, b3t = _in_proj_prompt(x_prompt, g_pre, w_rm, w_kvt)
    oa, og1, og2, og3 = _attention_prompt(sinks_a, qa, kva, qkv1, qkv2, qkv3)
    x1 = _out_proj_prompt(oa, og1, og2, og3, gates, x_prompt, wba, wbb, wout, g_post)
    y_prompt, tail = _ffn_prompt(x1, gf_pre, gf_post, wup, cw, cb, wdown)
    new_conv_p = tail[:, 6:8, :][None]

    nseq, tnew, d = x_sample.shape
    assert tnew == 4
    xs = x_sample.reshape(nseq * tnew, d)
    qbd, kvt_new, gates_s = _in_proj_sample(xs, g_pre, w_rm, w_kvt)
    ca, cb1, cb2, cb3 = (_to_feature_major(c[0]) for c in
                         (cache_a_kv, cache_b1_kv, cache_b2_kv, cache_b3_kv))
    na, nb1, nb2, nb3, o_s = _attention_sample(qbd, kvt_new, ca, cb1, cb2, cb3, sinks_a[0])
    o_s = o_s[:, 0:4, :].reshape(nseq * tnew, 768)
    x1s = _out_proj_sample(o_s, gates_s, xs, wba, wbb, wout, g_post)
    st = state_conv[0]
    zero = jnp.zeros_like(st[:, 0])
    e1 = jnp.stack([st[:, 1], zero, zero, zero], axis=1).reshape(nseq * tnew, D_FF)
    e2 = jnp.stack([st[:, 0], st[:, 1], zero, zero], axis=1).reshape(nseq * tnew, D_FF)
    ys, a_s = _ffn_sample(x1s, e1, e2, gf_pre, gf_post, wup, cw, cb, wdown)
    y_sample = ys.reshape(nseq, tnew, d)
    new_conv_s = a_s.reshape(nseq, tnew, D_FF)[:, 2:4, :][None]

    return (y_prompt, y_sample,
            _from_feature_major(at, A_KV_HEADS), _from_feature_major(na, A_KV_HEADS),
            _from_feature_major(b1t, 4), _from_feature_major(nb1, 4),
            _from_feature_major(b2t, 4), _from_feature_major(nb2, 4),
            _from_feature_major(b3t, 4), _from_feature_major(nb3, 4),
            new_conv_p, new_conv_s)
```

```python
import functools

import jax
import jax.numpy as jnp
import numpy as np
from jax import lax
from jax.experimental import pallas as pl
from jax.experimental.pallas import tpu as pltpu

F32 = jnp.float32
BF16 = jnp.bfloat16

HEAD_DIM = 64
A_HEADS = 8
A_KV_HEADS = 2
A_WINDOW = 128
B_GROUPS = ((128, 1), (512, 4), (2048, 16))
B_HEADS_PER_GROUP = 4
D_MODEL = 1024
D_FF = 4096
RMS_EPS = 1e-6
NEG_INF = -1e30
Q_SCALE = HEAD_DIM ** -0.5

LANES = 128
BLOCK = 128
TOKEN_BLOCK = 512
GROUP_COLS = B_HEADS_PER_GROUP * HEAD_DIM
SAMPLE_TILE_SEQS = 32

C_QA = (0, 512)
C_KVA = (512, 1024)
C_QKV1 = (1024, 1792)
C_GATES = (1792, 3840)
C_QKV2 = (3840, 4608)
C_QKV3 = (4608, 5376)
R_A = (0, 256)
R_B1 = (256, 768)
R_B2 = (768, 1280)
R_B3 = (1280, 1792)

VMEM_LIMIT = 56 * 1024 * 1024

_NT = (((1,), (1,)), ((), ()))


def _const_spec(shape):
    nd = len(shape)
    return pl.BlockSpec(shape, lambda *_: (0,) * nd, pipeline_mode=pl.Buffered(1))


def _rms(x, gain):
    ms = jnp.mean(x * x, axis=-1, keepdims=True)
    return x * lax.rsqrt(ms + RMS_EPS) * gain


def _alibi_slopes(n_heads):
    return jnp.exp2(-8.0 * jnp.arange(1, n_heads + 1, dtype=F32) / n_heads)


def _prep_in_weights(w):
    qa, ka, va = w[:, 0:512], w[:, 512:640], w[:, 640:768]
    qb, kb, vb = w[:, 768:1536], w[:, 1536:2304], w[:, 2304:3072]
    gates = w[:, 3072:5120]
    g = lambda z, i: z[:, GROUP_COLS * i:GROUP_COLS * (i + 1)]
    hd = HEAD_DIM
    kva2 = [z[:, i * hd:(i + 1) * hd] for z in (ka, va) for i in (0, 0, 1, 1)]
    w_rm = jnp.concatenate(
        [qa] + kva2 + [g(qb, 0), g(kb, 0), g(vb, 0), gates,
         g(qb, 1), g(kb, 1), g(vb, 1), g(qb, 2), g(kb, 2), g(vb, 2)], axis=1).astype(BF16)
    w_kvt = jnp.concatenate(
        [ka, va, g(kb, 0), g(vb, 0), g(kb, 1), g(vb, 1), g(kb, 2), g(vb, 2)], axis=1).T.astype(BF16)
    return w_rm, w_kvt


def _store_qkv(ref, r):
    ref[:, 0:GROUP_COLS] = (r[:, 0:GROUP_COLS] * Q_SCALE).astype(BF16)
    ref[:, GROUP_COLS:] = r[:, GROUP_COLS:].astype(BF16)


def _k1p_kernel(x_ref, g_ref, wrm_ref, wkvt_ref,
                qa_ref, kva_ref, qkv1_ref, gates_ref, qkv2_ref, qkv3_ref,
                at_ref, b1t_ref, b2t_ref, b3t_ref, slab_ref):
    tb = pl.program_id(1)
    gain = g_ref[...]

    def proj(hh, c):
        return jnp.dot(hh, wrm_ref[:, c[0]:c[1]], preferred_element_type=F32)

    h = _rms(x_ref[0], gain).astype(BF16)
    qa_ref[0] = (proj(h, C_QA) * Q_SCALE).astype(BF16)
    kva_ref[0] = proj(h, C_KVA).astype(BF16)
    _store_qkv(qkv1_ref.at[0], proj(h, C_QKV1))
    half = (C_GATES[0] + C_GATES[1]) // 2
    gates_ref[0, :, 0:D_MODEL] = proj(h, (C_GATES[0], half)).astype(BF16)
    gates_ref[0, :, D_MODEL:] = proj(h, (half, C_GATES[1])).astype(BF16)

    n_slabs = D_MODEL // LANES
    for s in range(n_slabs):
        slab_ref[s] = x_ref[0, :, s * LANES:(s + 1) * LANES]
    for dil, cols, out_ref in ((4, C_QKV2, qkv2_ref), (16, C_QKV3, qkv3_ref)):
        per = TOKEN_BLOCK // dil
        xs = jnp.concatenate(
            [jnp.concatenate([slab_ref[s, pl.ds(r, per, stride=dil), :] for r in range(dil)], axis=0)
             for s in range(n_slabs)], axis=1)
        hp = _rms(xs, gain).astype(BF16)
        rp = proj(hp, cols)
        out_ref[0, :, :, 0:GROUP_COLS] = (
            (rp[:, 0:GROUP_COLS] * Q_SCALE).astype(BF16).reshape(dil, per, GROUP_COLS))
        out_ref[0, :, :, GROUP_COLS:] = (
            rp[:, GROUP_COLS:].astype(BF16).reshape(dil, per, 2 * GROUP_COLS))

    b3t_ref[0] = lax.dot_general(wkvt_ref[R_B3[0]:R_B3[1], :], h, _NT, preferred_element_type=F32)

    @pl.when(tb == pl.num_programs(1) - 1)
    def _():
        b2t_ref[0] = lax.dot_general(wkvt_ref[R_B2[0]:R_B2[1], :], h, _NT, preferred_element_type=F32)
        t = lax.dot_general(wkvt_ref[R_A[0]:R_B1[1], :], h[TOKEN_BLOCK - BLOCK:, :], _NT,
                            preferred_element_type=F32)
        at_ref[0] = t[R_A[0]:R_A[1]]
        b1t_ref[0] = t[R_B1[0]:R_B1[1]]


def _in_proj_prompt(x, gain, w_rm, w_kvt):
    n, s, d = x.shape
    assert s % TOKEN_BLOCK == 0 and TOKEN_BLOCK == B_GROUPS[1][0] and s == B_GROUPS[2][0]
    nb = s // TOKEN_BLOCK
    out_shape = (
        jax.ShapeDtypeStruct((n, s, 512), BF16),
        jax.ShapeDtypeStruct((n, s, 512), BF16),
        jax.ShapeDtypeStruct((n, s, 768), BF16),
        jax.ShapeDtypeStruct((n, s, 2 * D_MODEL), BF16),
        jax.ShapeDtypeStruct((n, 4, s // 4, 768), BF16),
        jax.ShapeDtypeStruct((n, 16, s // 16, 768), BF16),
        jax.ShapeDtypeStruct((n, 256, BLOCK), F32),
        jax.ShapeDtypeStruct((n, 512, BLOCK), F32),
        jax.ShapeDtypeStruct((n, 512, TOKEN_BLOCK), F32),
        jax.ShapeDtypeStruct((n, 512, s), F32),
    )
    tok = lambda w: pl.BlockSpec((1, TOKEN_BLOCK, w), lambda i, j: (i, j, 0))
    out_specs = (
        tok(512), tok(512), tok(768), tok(2 * D_MODEL),
        pl.BlockSpec((1, 4, TOKEN_BLOCK // 4, 768), lambda i, j: (i, 0, j, 0)),
        pl.BlockSpec((1, 16, TOKEN_BLOCK // 16, 768), lambda i, j: (i, 0, j, 0)),
        pl.BlockSpec((1, 256, BLOCK), lambda i, j: (i, 0, 0)),
        pl.BlockSpec((1, 512, BLOCK), lambda i, j: (i, 0, 0)),
        pl.BlockSpec((1, 512, TOKEN_BLOCK), lambda i, j: (i, 0, 0)),
        pl.BlockSpec((1, 512, TOKEN_BLOCK), lambda i, j: (i, 0, j)),
    )
    return pl.pallas_call(
        _k1p_kernel,
        grid=(n, nb),
        in_specs=[tok(d), _const_spec((1, d)), _const_spec(w_rm.shape), _const_spec(w_kvt.shape)],
        out_specs=out_specs,
        out_shape=out_shape,
        scratch_shapes=[pltpu.VMEM((d // LANES, TOKEN_BLOCK, LANES), F32)],
        compiler_params=pltpu.CompilerParams(
            dimension_semantics=("arbitrary", "arbitrary"), vmem_limit_bytes=VMEM_LIMIT),
        name="in_proj_prompt",
    )(x, gain, w_rm, w_kvt)


def _prompt_bias_tables():
    i = jnp.arange(BLOCK)[:, None]
    j = jnp.arange(2 * BLOCK)[None, :]
    dist = (i + BLOCK) - j
    tabs = []
    slopes_a, slopes_b = _alibi_slopes(A_HEADS), _alibi_slopes(3 * B_HEADS_PER_GROUP)
    valid_a = (dist >= 0) & (dist <= A_WINDOW - 1)
    for h in range(A_HEADS):
        tabs.append(jnp.where(valid_a, -slopes_a[h] * dist.astype(F32), NEG_INF))
    for g, (win, dil) in enumerate(B_GROUPS):
        valid = (dist >= 0) & (dist <= win // dil)
        for hh in range(B_HEADS_PER_GROUP):
            s = slopes_b[g * B_HEADS_PER_GROUP + hh]
            tabs.append(jnp.where(valid, -s * (dist * dil).astype(F32), NEG_INF))
    return jnp.stack(tabs, axis=0).astype(F32)


def _band_unit(q_pairs, kcs, vcs, kps, vps, bias, no_prev, sink):
    low = lax.broadcasted_iota(jnp.int32, (BLOCK, 2 * HEAD_DIM), 1) < HEAD_DIM
    ones = jnp.ones((BLOCK, 2 * HEAD_DIM), BF16)
    q2s = []
    for qp in q_pairs:
        zero = jnp.zeros_like(qp)
        q2s.append(jnp.concatenate([jnp.where(low, qp, zero), jnp.where(low, zero, qp)], axis=0))
    nt = lambda a, b: lax.dot_general(a, b, _NT, preferred_element_type=F32)
    sc = jnp.concatenate([nt(q2, kc) for q2, kc in zip(q2s, kcs)], axis=0) + bias[:, BLOCK:]
    if kps is not None:
        bp = jnp.where(no_prev, NEG_INF, bias[:, :BLOCK])
        sp = jnp.concatenate([nt(q2, kp) for q2, kp in zip(q2s, kps)], axis=0) + bp
        m = jnp.max(jnp.maximum(sc, sp), axis=-1, keepdims=True)
    else:
        m = jnp.max(sc, axis=-1, keepdims=True)
    if sink is not None:
        m = jnp.maximum(m, sink[:, 0:1])
    pc = jnp.exp(sc - m).astype(BF16)
    l = jnp.dot(pc, ones, preferred_element_type=F32)
    o = [jnp.dot(pc[2 * BLOCK * p:2 * BLOCK * (p + 1)], vcs[p], preferred_element_type=F32)
         for p in range(2)]
    if kps is not None:
        pp = jnp.exp(sp - m).astype(BF16)
        l = l + jnp.dot(pp, ones, preferred_element_type=F32)
        o = [o[p] + jnp.dot(pp[2 * BLOCK * p:2 * BLOCK * (p + 1)], vps[p],
                            preferred_element_type=F32) for p in range(2)]
    if sink is not None:
        l = l + jnp.exp(sink - m)
    o = jnp.concatenate(o, axis=0) / l
    lse = m + jnp.log(l)
    pick = lambda a, p: jnp.where(low, a[2 * BLOCK * p:2 * BLOCK * p + BLOCK],
                                  a[2 * BLOCK * p + BLOCK:2 * BLOCK * (p + 1)])
    return [pick(o, 0), pick(o, 1)], [pick(lse, 0), pick(lse, 1)]


def _k2p_kernel(sink_ref, bias_ref,
                qa_ref, kva_ref, kvap_ref,
                q1_ref, q1p_ref, q2_ref, q2p_ref, q3_ref,
                oa_ref, og1_ref, og2_ref, og3_ref):
    j = pl.program_id(1)
    pw = 2 * HEAD_DIM
    first_blk = j == 0
    first_sub = (j % 4) == 0

    outs = []
    for g in range(A_KV_HEADS):
        sl = lambda ref, part: ref[0, :, part * 2 * pw + g * pw:part * 2 * pw + (g + 1) * pw]
        o, _ = _band_unit([qa_ref[0, :, (2 * g + p) * pw:(2 * g + p + 1) * pw] for p in range(2)],
                          [sl(kva_ref, 0)] * 2, [sl(kva_ref, 1)] * 2,
                          [sl(kvap_ref, 0)] * 2, [sl(kvap_ref, 1)] * 2,
                          bias_ref[g], first_blk, sink_ref[g])
        outs += o
    oa_ref[0] = jnp.concatenate(outs, axis=-1).astype(BF16)

    def group(cur, prev, unit, no_prev, out_ref):
        sl = lambda ref, part: [ref[:, part * GROUP_COLS + p * pw:part * GROUP_COLS + (p + 1) * pw]
                                for p in range(2)]
        kps = vps = None
        if prev is not None:
            kps, vps = sl(prev, 1), sl(prev, 2)
        o, lse = _band_unit(sl(cur, 0), sl(cur, 1), sl(cur, 2), kps, vps,
                            bias_ref[unit], no_prev, None)
        out_ref[...] = jnp.concatenate(o + lse, axis=-1)

    group(q1_ref.at[0], q1p_ref.at[0], 2, first_blk, og1_ref.at[0])
    group(q2_ref.at[0, 0], q2p_ref.at[0, 0], 3, first_sub, og2_ref.at[0, 0])
    group(q3_ref.at[0, 0], None, 4, None, og3_ref.at[0, 0])


def _attention_prompt(sinks, qa, kva, qkv1, qkv2, qkv3):
    n, s, _ = qa.shape
    nb = s // BLOCK
    assert nb == 16 and qkv2.shape[2] // BLOCK == 4 and qkv3.shape[2] == BLOCK
    bias = _prompt_bias_tables().reshape(5, 4 * BLOCK, 2 * BLOCK)
    sinks = jnp.broadcast_to(jnp.repeat(sinks.reshape(A_KV_HEADS, 4), BLOCK, axis=1)[:, :, None],
                             (A_KV_HEADS, 4 * BLOCK, 2 * HEAD_DIM)).astype(F32)
    prev = lambda j: jnp.maximum(j - 1, 0)
    in_specs = [
        _const_spec(sinks.shape),
        _const_spec(bias.shape),
        pl.BlockSpec((1, BLOCK, 512), lambda i, j: (i, j, 0)),
        pl.BlockSpec((1, BLOCK, 512), lambda i, j: (i, j, 0)),
        pl.BlockSpec((1, BLOCK, 512), lambda i, j: (i, prev(j), 0)),
        pl.BlockSpec((1, BLOCK, 768), lambda i, j: (i, j, 0)),
        pl.BlockSpec((1, BLOCK, 768), lambda i, j: (i, prev(j), 0)),
        pl.BlockSpec((1, 1, BLOCK, 768), lambda i, j: (i, j // 4, j % 4, 0)),
        pl.BlockSpec((1, 1, BLOCK, 768), lambda i, j: (i, j // 4, prev(j % 4), 0)),
        pl.BlockSpec((1, 1, BLOCK, 768), lambda i, j: (i, j, 0, 0)),
    ]
    out_shape = (
        jax.ShapeDtypeStruct((n, s, 512), BF16),
        jax.ShapeDtypeStruct((n, s, 512), F32),
        jax.ShapeDtypeStruct((n, 4, s // 4, 512), F32),
        jax.ShapeDtypeStruct((n, 16, s // 16, 512), F32),
    )
    out_specs = (
        pl.BlockSpec((1, BLOCK, 512), lambda i, j: (i, j, 0)),
        pl.BlockSpec((1, BLOCK, 512), lambda i, j: (i, j, 0)),
        pl.BlockSpec((1, 1, BLOCK, 512), lambda i, j: (i, j // 4, j % 4, 0)),
        pl.BlockSpec((1, 1, BLOCK, 512), lambda i, j: (i, j, 0, 0)),
    )
    return pl.pallas_call(
        _k2p_kernel,
        grid=(n, nb),
        in_specs=in_specs,
        out_specs=out_specs,
        out_shape=out_shape,
        compiler_params=pltpu.CompilerParams(
            dimension_semantics=("arbitrary", "arbitrary"), vmem_limit_bytes=VMEM_LIMIT),
        name="attention_prompt",
    )(sinks, bias, qa, kva, kva, qkv1, qkv1, qkv2, qkv2, qkv3)


def _sigmoid(x):
    return 1.0 / (1.0 + jnp.exp(-x))


def _mix_tail(oa, ob, gates_ref, x_ref, wba_ref, wbb_ref, wout_ref, gpost_ref, out_ref):
    ya = jnp.dot(oa, wba_ref[...], preferred_element_type=F32)
    yb = jnp.dot(ob, wbb_ref[...], preferred_element_type=F32)
    ga = gates_ref[:, 0:D_MODEL].astype(F32)
    gb = gates_ref[:, D_MODEL:].astype(F32)
    mixed = _sigmoid(ga) * ya + _sigmoid(gb) * yb
    mix = jnp.dot(mixed.astype(BF16), wout_ref[...], preferred_element_type=F32)
    out_ref[...] = x_ref[...] + _rms(mix, gpost_ref[...])


def _k3p_kernel(oa_ref, og1_ref, og2_ref, og3_ref, gates_ref, x_ref,
                wba_ref, wbb_ref, wout_ref, gpost_ref, out_ref, s2_ref, s3_ref):
    n_slabs = 2 * GROUP_COLS // LANES
    for dil, src, dst in ((4, og2_ref, s2_ref), (16, og3_ref, s3_ref)):
        for r in range(dil):
            for s in range(n_slabs):
                dst[s, pl.ds(r, TOKEN_BLOCK // dil, stride=dil), :] = (
                    src[0, r, :, s * LANES:(s + 1) * LANES])
    o1, l1 = og1_ref[0, :, 0:GROUP_COLS], og1_ref[0, :, GROUP_COLS:]
    o2 = jnp.concatenate([s2_ref[0], s2_ref[1]], axis=1)
    l2 = jnp.concatenate([s2_ref[2], s2_ref[3]], axis=1)
    o3 = jnp.concatenate([s3_ref[0], s3_ref[1]], axis=1)
    l3 = jnp.concatenate([s3_ref[2], s3_ref[3]], axis=1)
    m = jnp.maximum(jnp.maximum(l1, l2), l3)
    e1, e2, e3 = jnp.exp(l1 - m), jnp.exp(l2 - m), jnp.exp(l3 - m)
    ob = (e1 * o1 + e2 * o2 + e3 * o3) / (e1 + e2 + e3)
    _mix_tail(oa_ref[0], ob.astype(BF16), gates_ref.at[0], x_ref.at[0],
              wba_ref, wbb_ref, wout_ref, gpost_ref, out_ref.at[0])


def _out_proj_prompt(oa, og1, og2, og3, gates, x, wba, wbb, wout, gpost):
    n, s, d = x.shape
    nb = s // TOKEN_BLOCK
    tok = lambda w: pl.BlockSpec((1, TOKEN_BLOCK, w), lambda i, j: (i, j, 0))
    return pl.pallas_call(
        _k3p_kernel,
        grid=(n, nb),
        in_specs=[
            tok(512), tok(512),
            pl.BlockSpec((1, 4, TOKEN_BLOCK // 4, 512), lambda i, j: (i, 0, j, 0)),
            pl.BlockSpec((1, 16, TOKEN_BLOCK // 16, 512), lambda i, j: (i, 0, j, 0)),
            tok(2 * D_MODEL), tok(d),
            _const_spec(wba.shape), _const_spec(wbb.shape), _const_spec(wout.shape),
            _const_spec(gpost.shape),
        ],
        out_specs=tok(d),
        out_shape=jax.ShapeDtypeStruct((n, s, d), F32),
        scratch_shapes=[pltpu.VMEM((2 * GROUP_COLS // LANES, TOKEN_BLOCK, LANES), F32)] * 2,
        compiler_params=pltpu.CompilerParams(
            dimension_semantics=("arbitrary", "arbitrary"), vmem_limit_bytes=VMEM_LIMIT),
        name="out_proj_prompt",
    )(oa, og1, og2, og3, gates, x, wba, wbb, wout, gpost)


def _k3s_kernel(o_ref, gates_ref, x_ref, wba_ref, wbb_ref, wout_ref, gpost_ref, out_ref):
    oa = o_ref[:, 0:512].astype(BF16)
    ob = o_ref[:, 512:768].astype(BF16)
    _mix_tail(oa, ob, gates_ref, x_ref, wba_ref, wbb_ref, wout_ref, gpost_ref, out_ref)


def _out_proj_sample(o, gates, x, wba, wbb, wout, gpost):
    t, d = x.shape
    full = lambda a: pl.BlockSpec(a.shape, lambda i: (0,) * a.ndim)
    args = (o, gates, x, wba, wbb, wout, gpost)
    return pl.pallas_call(
        _k3s_kernel,
        grid=(1,),
        in_specs=[full(a) for a in args],
        out_specs=pl.BlockSpec((t, d), lambda i: (0, 0)),
        out_shape=jax.ShapeDtypeStruct((t, d), F32),
        compiler_params=pltpu.CompilerParams(
            dimension_semantics=("arbitrary",), vmem_limit_bytes=VMEM_LIMIT),
        name="out_proj_sample",
    )(*args)


FF_CHUNK = 1024


def _gelu_tanh(c):
    return 0.5 * c * (1.0 + jnp.tanh(np.sqrt(2.0 / np.pi).astype(np.float32)
                                     * (c + 0.044715 * (c * c * c))))


def _ffn_chunk(h, w_a, w_g, cw, cb, w_down, prev_fn):
    a = jnp.dot(h, w_a, preferred_element_type=F32)
    g = jnp.dot(h, w_g, preferred_element_type=F32)
    p1, p2 = prev_fn(a)
    conv = cb + cw[0:1] * p2 + cw[1:2] * p1 + cw[2:3] * a
    act = (_gelu_tanh(conv) * g).astype(BF16)
    return jnp.dot(act, w_down, preferred_element_type=F32), a


def _k4p_kernel(x_ref, gpre_ref, gpost_ref, wup_ref, cw_ref, cb_ref, wdown_ref,
                out_ref, tail_ref, hist_ref):
    tb = pl.program_id(1)
    tq = TOKEN_BLOCK

    @pl.when(tb == 0)
    def _():
        hist_ref[:, 0:8, :] = jnp.zeros((hist_ref.shape[0], 8, LANES), F32)

    x = x_ref[0]
    h = _rms(x, gpre_ref[...]).astype(BF16)
    y = jnp.zeros((tq, D_MODEL), F32)
    for c0 in range(0, D_FF, FF_CHUNK):
        c1 = c0 + FF_CHUNK
        slabs = range(c0 // LANES, c1 // LANES)

        def prev(a, c0=c0, slabs=slabs):
            for j in slabs:
                hist_ref[j, 8:8 + tq, :] = a[:, j * LANES - c0:(j + 1) * LANES - c0]
            p1 = jnp.concatenate([hist_ref[j, 7:7 + tq, :] for j in slabs], axis=1)
            p2 = jnp.concatenate([hist_ref[j, 6:6 + tq, :] for j in slabs], axis=1)
            return p1, p2

        yc, a = _ffn_chunk(h, wup_ref[:, c0:c1], wup_ref[:, D_FF + c0:D_FF + c1],
                           cw_ref[:, c0:c1], cb_ref[:, c0:c1], wdown_ref[c0:c1, :], prev)
        tail_ref[0, :, c0:c1] = a[tq - 8:, :]
        for j in slabs:
            hist_ref[j, 0:8, :] = a[tq - 8:, j * LANES - c0:(j + 1) * LANES - c0]
        y = y + yc
    out_ref[0] = x + _rms(y, gpost_ref[...])


def _ffn_prompt(x, gpre, gpost, wup, cw, cb, wdown):
    n, s, d = x.shape
    nb = s // TOKEN_BLOCK
    tok = pl.BlockSpec((1, TOKEN_BLOCK, d), lambda i, j: (i, j, 0))
    consts = (gpre, gpost, wup, cw, cb, wdown)
    return pl.pallas_call(
        _k4p_kernel,
        grid=(n, nb),
        in_specs=[tok] + [_const_spec(a.shape) for a in consts],
        out_specs=(tok, pl.BlockSpec((1, 8, D_FF), lambda i, j: (i, 0, 0))),
        out_shape=(jax.ShapeDtypeStruct((n, s, d), F32), jax.ShapeDtypeStruct((n, 8, D_FF), F32)),
        scratch_shapes=[pltpu.VMEM((D_FF // LANES, 8 + TOKEN_BLOCK, LANES), F32)],
        compiler_params=pltpu.CompilerParams(
            dimension_semantics=("arbitrary", "arbitrary"), vmem_limit_bytes=VMEM_LIMIT),
        name="ffn_prompt",
    )(x, *consts)


def _k4s_kernel(x_ref, e1_ref, e2_ref, gpre_ref, gpost_ref, wa_ref, wg_ref, cw_ref, cb_ref,
                wdown_ref, out_ref, a_ref, h_ref, y_ref):
    c = pl.program_id(0)

    @pl.when(c == 0)
    def _():
        h_ref[...] = _rms(x_ref[...], gpre_ref[...]).astype(BF16)
        y_ref[...] = jnp.zeros_like(y_ref)

    pos = lax.broadcasted_iota(jnp.int32, a_ref.shape, 0) % 4

    def prev(a):
        return (jnp.where(pos == 0, e1_ref[...], pltpu.roll(a, 1, axis=0)),
                jnp.where(pos < 2, e2_ref[...], pltpu.roll(a, 2, axis=0)))

    yc, a = _ffn_chunk(h_ref[...], wa_ref[...], wg_ref[...], cw_ref[...], cb_ref[...],
                       wdown_ref[...], prev)
    a_ref[...] = a
    y_ref[...] += yc

    @pl.when(c == pl.num_programs(0) - 1)
    def _():
        out_ref[...] = x_ref[...] + _rms(y_ref[...], gpost_ref[...])


def _ffn_sample(x, e1, e2, gpre, gpost, wup, cw, cb, wdown):
    t, d = x.shape
    nc = D_FF // FF_CHUNK
    const = lambda a: pl.BlockSpec(a.shape, lambda c: (0,) * a.ndim)
    cols = lambda rows: pl.BlockSpec((rows, FF_CHUNK), lambda c: (0, c))
    return pl.pallas_call(
        _k4s_kernel,
        grid=(nc,),
        in_specs=[const(x), cols(t), cols(t), const(gpre), const(gpost),
                  cols(d), pl.BlockSpec((d, FF_CHUNK), lambda c: (0, nc + c)),
                  cols(3), cols(1), pl.BlockSpec((FF_CHUNK, d), lambda c: (c, 0))],
        out_specs=(pl.BlockSpec((t, d), lambda c: (0, 0)), cols(t)),
        out_shape=(jax.ShapeDtypeStruct((t, d), F32), jax.ShapeDtypeStruct((t, D_FF), F32)),
        scratch_shapes=[pltpu.VMEM((t, d), BF16), pltpu.VMEM((t, d), F32)],
        compiler_params=pltpu.CompilerParams(
            dimension_semantics=("arbitrary",), vmem_limit_bytes=VMEM_LIMIT),
        name="ffn_sample",
    )(x, e1, e2, gpre, gpost, wup, wup, cw, cb, wdown)


def _k1s_kernel(x_ref, g_ref, wrm_ref, wkvt_ref, rep_ref, qbd_ref, kvt_ref, gates_ref):
    h = _rms(x_ref[...], g_ref[...]).astype(BF16)
    shape = (rep_ref.shape[0], GROUP_COLS)
    own = ((lax.broadcasted_iota(jnp.int32, shape, 0) % 16) // 4
           == lax.broadcasted_iota(jnp.int32, shape, 1) // HEAD_DIM)
    starts = (C_QA[0], C_QA[0] + GROUP_COLS, C_QKV1[0], C_QKV2[0], C_QKV3[0])
    for u, c0 in enumerate(starts):
        q = jnp.dot(h, wrm_ref[:, c0:c0 + GROUP_COLS], preferred_element_type=F32)
        q = (q * Q_SCALE).astype(BF16)
        qrep = jnp.dot(rep_ref[...], q, preferred_element_type=F32)
        qbd_ref[:, u * GROUP_COLS:(u + 1) * GROUP_COLS] = jnp.where(own, qrep, 0.0).astype(BF16)
    kvt_ref[...] = lax.dot_general(wkvt_ref[...], h, _NT, preferred_element_type=F32)
    half = (C_GATES[0] + C_GATES[1]) // 2
    gates_ref[:, 0:D_MODEL] = jnp.dot(h, wrm_ref[:, C_GATES[0]:half],
                                      preferred_element_type=F32).astype(BF16)
    gates_ref[:, D_MODEL:] = jnp.dot(h, wrm_ref[:, half:C_GATES[1]],
                                     preferred_element_type=F32).astype(BF16)


def _in_proj_sample(x, gain, w_rm, w_kvt):
    t, d = x.shape
    r = jnp.arange(4 * t)
    src = (r // 16) * 4 + r % 4
    rep = (src[:, None] == jnp.arange(t)[None, :]).astype(BF16)
    args = (x, gain, w_rm, w_kvt, rep)
    full = lambda a: pl.BlockSpec(a.shape, lambda i: (0,) * a.ndim, pipeline_mode=pl.Buffered(1))
    out_shape = (
        jax.ShapeDtypeStruct((4 * t, 1280), BF16),
        jax.ShapeDtypeStruct((w_kvt.shape[0], t), F32),
        jax.ShapeDtypeStruct((t, 2 * D_MODEL), BF16),
    )
    return pl.pallas_call(
        _k1s_kernel,
        grid=(1,),
        in_specs=[full(a) for a in args],
        out_specs=tuple(pl.BlockSpec(s.shape, lambda i: (0, 0)) for s in out_shape),
        out_shape=out_shape,
        compiler_params=pltpu.CompilerParams(
            dimension_semantics=("arbitrary",), vmem_limit_bytes=VMEM_LIMIT),
        name="in_proj_sample",
    )(*args)


_UNIT_COLS = ((0, 256), (256, 512), (512, 768), (768, 1024), (1024, 1280))
_UNIT_W = (A_WINDOW, A_WINDOW, B_GROUPS[0][0], B_GROUPS[1][0], B_GROUPS[2][0])
_UNIT_BIAS_OFF = tuple(int(v) for v in np.cumsum((0,) + _UNIT_W)[:-1])


def _sample_tables(sinks):
    slopes_a, slopes_b = _alibi_slopes(A_HEADS), _alibi_slopes(3 * B_HEADS_PER_GROUP)
    row = jnp.arange(16)
    slot, tok = row // 4, row % 4
    unit_slopes = [slopes_a[slot], slopes_a[4 + slot],
                   slopes_b[slot], slopes_b[4 + slot], slopes_b[8 + slot]]
    dils = (1, 1, 1, 4, 16)
    maxd = (A_WINDOW - 1, A_WINDOW - 1) + tuple(w for w, _ in B_GROUPS)
    bias = []
    for u in range(5):
        w = _UNIT_W[u]
        dist = (w + tok)[:, None] - jnp.arange(w)[None, :]
        valid = (dist % dils[u] == 0) & (dist <= maxd[u])
        bias.append(jnp.where(valid, -unit_slopes[u][:, None] * dist.astype(F32), NEG_INF))
    bias = jnp.concatenate(bias, axis=1).astype(F32)
    slope = jnp.broadcast_to(jnp.stack(unit_slopes)[:, :, None], (5, 16, 128)).astype(F32)
    sink = jnp.stack([sinks[slot], sinks[4 + slot]])
    sink = jnp.broadcast_to(sink[:, :, None], (2, 16, 128)).astype(F32)
    return bias, slope, sink


def _k2s_kernel(qbd_ref, new_ref, ca_ref, cb1_ref, cb2_ref, cb3_ref, bias_ref, slope_ref, sink_ref,
                na_ref, nb1_ref, nb2_ref, nb3_ref, o_ref):
    nl = pl.program_id(0) % SAMPLE_TILE_SEQS
    row = lax.broadcasted_iota(jnp.int32, (16, 128), 0)
    lane = lax.broadcasted_iota(jnp.int32, (16, 128), 1)
    tok, ltok = row % 4, lane % 4
    mine = (lane // 4) == nl
    dist_new = (tok - ltok).astype(F32)
    own = (lax.broadcasted_iota(jnp.int32, (16, GROUP_COLS), 0) // 4
           == lax.broadcasted_iota(jnp.int32, (16, GROUP_COLS), 1) // HEAD_DIM)
    pick = (lax.broadcasted_iota(jnp.int32, (8, 16), 1) % 4
            == lax.broadcasted_iota(jnp.int32, (8, 16), 0)).astype(BF16)

    def scores(u, kt, knew):
        q = qbd_ref[:, _UNIT_COLS[u][0]:_UNIT_COLS[u][1]]
        w = _UNIT_W[u]
        off = _UNIT_BIAS_OFF[u]
        s = jnp.dot(q, kt.astype(BF16), preferred_element_type=F32) + bias_ref[:, off:off + w]
        ok = mine & ((ltok <= tok) if u < 3 else (ltok == tok))
        bn = jnp.where(ok, -slope_ref[u] * dist_new, NEG_INF)
        sn = jnp.dot(q, knew.astype(BF16), preferred_element_type=F32) + bn
        m = jnp.maximum(jnp.max(s, axis=-1, keepdims=True), jnp.max(sn, axis=-1, keepdims=True))
        return s, sn, m

    def weighted(s, sn, m, vt, vnew):
        p, pn = jnp.exp(s - m), jnp.exp(sn - m)
        l = jnp.sum(p, axis=-1, keepdims=True) + jnp.sum(pn, axis=-1, keepdims=True)
        o = (lax.dot_general(p.astype(BF16), vt.astype(BF16), _NT, preferred_element_type=F32)
             + lax.dot_general(pn.astype(BF16), vnew.astype(BF16), _NT, preferred_element_type=F32))
        return o, l

    def finish(o, l):
        o = jnp.where(own, o / l, 0.0).astype(BF16)
        return jnp.dot(pick, o, preferred_element_type=F32)

    tile4 = lambda a: jnp.concatenate([a, a, a, a], axis=0)
    outs = []
    for g in range(A_KV_HEADS):
        kt = tile4(ca_ref[0, g * 64:(g + 1) * 64, :])
        vt = tile4(ca_ref[0, 128 + g * 64:128 + (g + 1) * 64, :])
        knew = tile4(new_ref[R_A[0] + g * 64:R_A[0] + (g + 1) * 64, :])
        vnew = tile4(new_ref[R_A[0] + 128 + g * 64:R_A[0] + 128 + (g + 1) * 64, :])
        s, sn, m = scores(g, kt, knew)
        sink = sink_ref[g][:, 0:1]
        m = jnp.maximum(m, sink)
        o, l = weighted(s, sn, m, vt, vnew)
        outs.append(finish(o, l + jnp.exp(sink - m)))

    parts = []
    for u, (c_ref, rows) in ((2, (cb1_ref, R_B1)), (3, (cb2_ref, R_B2)), (4, (cb3_ref, R_B3))):
        kt, vt = c_ref[0, 0:GROUP_COLS, :], c_ref[0, GROUP_COLS:, :]
        knew = new_ref[rows[0]:rows[0] + GROUP_COLS, :]
        vnew = new_ref[rows[0] + GROUP_COLS:rows[1], :]
        parts.append(scores(u, kt, knew) + (vt, vnew))
    m = functools.reduce(jnp.maximum, [p[2] for p in parts])
    o_sum, l_sum = 0.0, 0.0
    for s, sn, _, vt, vnew in parts:
        o, l = weighted(s, sn, m, vt, vnew)
        o_sum, l_sum = o_sum + o, l_sum + l
    outs.append(finish(o_sum, l_sum))
    o_ref[0] = jnp.concatenate(outs, axis=-1)

    lane_t = lax.broadcasted_iota(jnp.int32, (8, 128), 1)
    for c_ref, n_ref, rows in ((ca_ref, na_ref, R_A), (cb1_ref, nb1_ref, R_B1),
                               (cb2_ref, nb2_ref, R_B2), (cb3_ref, nb3_ref, R_B3)):
        w = c_ref.shape[2]
        nrows = rows[1] - rows[0]
        chunk = 256

        def body(i, carry, c_ref=c_ref, n_ref=n_ref, rows=rows, w=w):
            r0 = pl.multiple_of(i * chunk, chunk)
            x = c_ref[0, pl.ds(r0, chunk), :]
            shifted = pltpu.roll(x, w - 4, axis=1)
            new = new_ref[pl.ds(pl.multiple_of(rows[0] + r0, chunk), chunk), :]
            new = pltpu.roll(new, 124 - 4 * nl, axis=1)
            if w > 128:
                n_ref[0, pl.ds(r0, chunk), 0:w - 128] = shifted[:, 0:w - 128]
            sel = jnp.broadcast_to(lane_t[0:1] >= 124, (chunk, 128))
            n_ref[0, pl.ds(r0, chunk), w - 128:w] = jnp.where(sel, new, shifted[:, w - 128:w])
            return carry

        lax.fori_loop(0, nrows // chunk, body, 0)


def _attention_sample(qbd, kvt_new, ca, cb1, cb2, cb3, sinks):
    nseq = ca.shape[0]
    assert nseq % SAMPLE_TILE_SEQS == 0 and kvt_new.shape[1] == 4 * nseq
    assert (ca.shape[2], cb1.shape[2], cb2.shape[2], cb3.shape[2]) == _UNIT_W[1:]
    bias, slope, sink = _sample_tables(sinks)
    cache_spec = lambda a: pl.BlockSpec((1,) + a.shape[1:], lambda i: (i, 0, 0))
    in_specs = [
        pl.BlockSpec((16, 1280), lambda i: (i, 0)),
        pl.BlockSpec((kvt_new.shape[0], 128), lambda i: (0, i // SAMPLE_TILE_SEQS)),
        cache_spec(ca), cache_spec(cb1), cache_spec(cb2), cache_spec(cb3),
        _const_spec(bias.shape), _const_spec(slope.shape), _const_spec(sink.shape),
    ]
    out_shape = tuple(jax.ShapeDtypeStruct(a.shape, F32) for a in (ca, cb1, cb2, cb3)) + (
        jax.ShapeDtypeStruct((nseq, 8, 768), F32),)
    out_specs = tuple(cache_spec(a) for a in (ca, cb1, cb2, cb3)) + (
        pl.BlockSpec((1, 8, 768), lambda i: (i, 0, 0)),)
    return pl.pallas_call(
        _k2s_kernel,
        grid=(nseq,),
        in_specs=in_specs,
        out_specs=out_specs,
        out_shape=out_shape,
        compiler_params=pltpu.CompilerParams(
            dimension_semantics=("arbitrary",), vmem_limit_bytes=VMEM_LIMIT),
        name="attention_sample",
    )(qbd, kvt_new, ca, cb1, cb2, cb3, bias, slope, sink)


def _to_feature_major(cache):
    n, w = cache.shape[0], cache.shape[1]
    return jnp.transpose(cache, (0, 2, 3, 4, 1)).reshape(n, -1, w)


def _from_feature_major(c, heads):
    n, _, w = c.shape
    return jnp.transpose(c.reshape(n, 2, heads, HEAD_DIM, w), (0, 4, 1, 2, 3))[None]


def kernel(x_prompt, x_sample, cache_a_kv, cache_b1_kv, cache_b2_kv, cache_b3_kv, state_conv,
           w_in, sinks_a, w_branch_a, w_branch_b, w_out, norm_mix_pre, norm_mix_post,
           norm_ffn_pre, norm_ffn_post, w_up, conv_w, conv_b, w_down):
    assert w_in.shape[0] == 1, "single layer"
    w_rm, w_kvt = _prep_in_weights(w_in[0])
    wba, wbb, wout = (w_branch_a[0].astype(BF16), w_branch_b[0].astype(BF16), w_out[0].astype(BF16))
    wup, wdown = w_up[0].astype(BF16), w_down[0].astype(BF16)
    g_pre, g_post, gf_pre, gf_post = norm_mix_pre, norm_mix_post, norm_ffn_pre, norm_ffn_post
    cw, cb = conv_w[0], conv_b

    qa, kva, qkv1, gates, qkv2, qkv3, at, b1t, b2t, b3t = _in_proj_prompt(x_prompt, g_pre, w_rm, w_kvt)
    oa, og1, og2, og3 = _attention_prompt(sinks_a, qa, kva, qkv1, qkv2, qkv3)
    x1 = _out_proj_prompt(oa, og1, og2, og3, gates, x_prompt, wba, wbb, wout, g_post)
    y_prompt, tail = _ffn_prompt(x1, gf_pre, gf_post, wup, cw, cb, wdown)
    new_conv_p = tail[:, 6:8, :][None]

    nseq, tnew, d = x_sample.shape
    assert tnew == 4
    xs = x_sample.reshape(nseq * tnew, d)
    qbd, kvt_new, gates_s = _in_proj_sample(xs, g_pre, w_rm, w_kvt)
    ca, cb1, cb2, cb3 = (_to_feature_major(c[0]) for c in
                         (cache_a_kv, cache_b1_kv, cache_b2_kv, cache_b3_kv))
    na, nb1, nb2, nb3, o_s = _attention_sample(qbd, kvt_new, ca, cb1, cb2, cb3, sinks_a[0])
    o_s = o_s[:, 0:4, :].reshape(nseq * tnew, 768)
    x1s = _out_proj_sample(o_s, gates_s, xs, wba, wbb, wout, g_post)
    st = state_conv[0]
    zero = jnp.zeros_like(st[:, 0])
    e1 = jnp.stack([st[:, 1], zero, zero, zero], axis=1).reshape(nseq * tnew, D_FF)
    e2 = jnp.stack([st[:, 0], st[:, 1], zero, zero], axis=1).reshape(nseq * tnew, D_FF)
    ys, a_s = _ffn_sample(x1s, e1, e2, gf_pre, gf_post, wup, cw, cb, wdown)
    y_sample = ys.reshape(nseq, tnew, d)
    new_conv_s = a_s.reshape(nseq, tnew, D_FF)[:, 2:4, :][None]

    return (y_prompt, y_sample,
            _from_feature_major(at, A_KV_HEADS), _from_feature_major(na, A_KV_HEADS),
            _from_feature_major(b1t, 4), _from_feature_major(nb1, 4),
            _from_feature_major(b2t, 4), _from_feature_major(nb2, 4),
            _from_feature_major(b3t, 4), _from_feature_major(nb3, 4),
            new_conv_p, new_conv_s)
```

```python
import functools

import jax
import jax.numpy as jnp
import numpy as np
from jax import lax
from jax.experimental import pallas as pl
from jax.experimental.pallas import tpu as pltpu

F32 = jnp.float32
BF16 = jnp.bfloat16

HEAD_DIM = 64
A_HEADS = 8
A_KV_HEADS = 2
A_WINDOW = 128
B_GROUPS = ((128, 1), (512, 4), (2048, 16))
B_HEADS_PER_GROUP = 4
D_MODEL = 1024
D_FF = 4096
RMS_EPS = 1e-6
NEG_INF = -1e30
Q_SCALE = HEAD_DIM ** -0.5

LANES = 128
BLOCK = 128
TOKEN_BLOCK = 512
GROUP_COLS = B_HEADS_PER_GROUP * HEAD_DIM
SAMPLE_TILE_SEQS = 32

C_QA = (0, 512)
C_KVA = (512, 1024)
C_QKV1 = (1024, 1792)
C_GATES = (1792, 3840)
C_QKV2 = (3840, 4608)
C_QKV3 = (4608, 5376)
R_A = (0, 256)
R_B1 = (256, 768)
R_B2 = (768, 1280)
R_B3 = (1280, 1792)

VMEM_LIMIT = 56 * 1024 * 1024

_NT = (((1,), (1,)), ((), ()))


def _const_spec(shape):
    nd = len(shape)
    return pl.BlockSpec(shape, lambda *_: (0,) * nd, pipeline_mode=pl.Buffered(1))


def _rms(x, gain):
    ms = jnp.mean(x * x, axis=-1, keepdims=True)
    return x * lax.rsqrt(ms + RMS_EPS) * gain


def _alibi_slopes(n_heads):
    return jnp.exp2(-8.0 * jnp.arange(1, n_heads + 1, dtype=F32) / n_heads)


def _prep_in_weights(w):
    qa, ka, va = w[:, 0:512], w[:, 512:640], w[:, 640:768]
    qb, kb, vb = w[:, 768:1536], w[:, 1536:2304], w[:, 2304:3072]
    gates = w[:, 3072:5120]
    g = lambda z, i: z[:, GROUP_COLS * i:GROUP_COLS * (i + 1)]
    hd = HEAD_DIM
    kva2 = [z[:, i * hd:(i + 1) * hd] for z in (ka, va) for i in (0, 0, 1, 1)]
    w_rm = jnp.concatenate(
        [qa] + kva2 + [g(qb, 0), g(kb, 0), g(vb, 0), gates,
         g(qb, 1), g(kb, 1), g(vb, 1), g(qb, 2), g(kb, 2), g(vb, 2)], axis=1).astype(BF16)
    w_kvt = jnp.concatenate(
        [ka, va, g(kb, 0), g(vb, 0), g(kb, 1), g(vb, 1), g(kb, 2), g(vb, 2)], axis=1).T.astype(BF16)
    return w_rm, w_kvt


def _store_qkv(ref, r):
    ref[:, 0:GROUP_COLS] = (r[:, 0:GROUP_COLS] * Q_SCALE).astype(BF16)
    ref[:, GROUP_COLS:] = r[:, GROUP_COLS:].astype(BF16)


def _k1p_kernel(x_ref, g_ref, wrm_ref, wkvt_ref,
                qa_ref, kva_ref, qkv1_ref, gates_ref, qkv2_ref, qkv3_ref,
                at_ref, b1t_ref, b2t_ref, b3t_ref, slab_ref):
    tb = pl.program_id(1)
    gain = g_ref[...]

    def proj(hh, c):
        return jnp.dot(hh, wrm_ref[:, c[0]:c[1]], preferred_element_type=F32)

    h = _rms(x_ref[0], gain).astype(BF16)
    qa_ref[0] = (proj(h, C_QA) * Q_SCALE).astype(BF16)
    kva_ref[0] = proj(h, C_KVA).astype(BF16)
    _store_qkv(qkv1_ref.at[0], proj(h, C_QKV1))
    half = (C_GATES[0] + C_GATES[1]) // 2
    gates_ref[0, :, 0:D_MODEL] = proj(h, (C_GATES[0], half)).astype(BF16)
    gates_ref[0, :, D_MODEL:] = proj(h, (half, C_GATES[1])).astype(BF16)

    n_slabs = D_MODEL // LANES
    for s in range(n_slabs):
        slab_ref[s] = x_ref[0, :, s * LANES:(s + 1) * LANES]
    for dil, cols, out_ref in ((4, C_QKV2, qkv2_ref), (16, C_QKV3, qkv3_ref)):
        per = TOKEN_BLOCK // dil
        xs = jnp.concatenate(
            [jnp.concatenate([slab_ref[s, pl.ds(r, per, stride=dil), :] for r in range(dil)], axis=0)
             for s in range(n_slabs)], axis=1)
        hp = _rms(xs, gain).astype(BF16)
        rp = proj(hp, cols)
        out_ref[0, :, :, 0:GROUP_COLS] = (
            (rp[:, 0:GROUP_COLS] * Q_SCALE).astype(BF16).reshape(dil, per, GROUP_COLS))
        out_ref[0, :, :, GROUP_COLS:] = (
            rp[:, GROUP_COLS:].astype(BF16).reshape(dil, per, 2 * GROUP_COLS))

    b3t_ref[0] = lax.dot_general(wkvt_ref[R_B3[0]:R_B3[1], :], h, _NT, preferred_element_type=F32)

    @pl.when(tb == pl.num_programs(1) - 1)
    def _():
        b2t_ref[0] = lax.dot_general(wkvt_ref[R_B2[0]:R_B2[1], :], h, _NT, preferred_element_type=F32)
        t = lax.dot_general(wkvt_ref[R_A[0]:R_B1[1], :], h[TOKEN_BLOCK - BLOCK:, :], _NT,
                            preferred_element_type=F32)
        at_ref[0] = t[R_A[0]:R_A[1]]
        b1t_ref[0] = t[R_B1[0]:R_B1[1]]


def _in_proj_prompt(x, gain, w_rm, w_kvt):
    n, s, d = x.shape
    assert s % TOKEN_BLOCK == 0 and TOKEN_BLOCK == B_GROUPS[1][0] and s == B_GROUPS[2][0]
    nb = s // TOKEN_BLOCK
    out_shape = (
        jax.ShapeDtypeStruct((n, s, 512), BF16),
        jax.ShapeDtypeStruct((n, s, 512), BF16),
        jax.ShapeDtypeStruct((n, s, 768), BF16),
        jax.ShapeDtypeStruct((n, s, 2 * D_MODEL), BF16),
        jax.ShapeDtypeStruct((n, 4, s // 4, 768), BF16),
        jax.ShapeDtypeStruct((n, 16, s // 16, 768), BF16),
        jax.ShapeDtypeStruct((n, 256, BLOCK), F32),
        jax.ShapeDtypeStruct((n, 512, BLOCK), F32),
        jax.ShapeDtypeStruct((n, 512, TOKEN_BLOCK), F32),
        jax.ShapeDtypeStruct((n, 512, s), F32),
    )
    tok = lambda w: pl.BlockSpec((1, TOKEN_BLOCK, w), lambda i, j: (i, j, 0))
    out_specs = (
        tok(512), tok(512), tok(768), tok(2 * D_MODEL),
        pl.BlockSpec((1, 4, TOKEN_BLOCK // 4, 768), lambda i, j: (i, 0, j, 0)),
        pl.BlockSpec((1, 16, TOKEN_BLOCK // 16, 768), lambda i, j: (i, 0, j, 0)),
        pl.BlockSpec((1, 256, BLOCK), lambda i, j: (i, 0, 0)),
        pl.BlockSpec((1, 512, BLOCK), lambda i, j: (i, 0, 0)),
        pl.BlockSpec((1, 512, TOKEN_BLOCK), lambda i, j: (i, 0, 0)),
        pl.BlockSpec((1, 512, TOKEN_BLOCK), lambda i, j: (i, 0, j)),
    )
    return pl.pallas_call(
        _k1p_kernel,
        grid=(n, nb),
        in_specs=[tok(d), _const_spec((1, d)), _const_spec(w_rm.shape), _const_spec(w_kvt.shape)],
        out_specs=out_specs,
        out_shape=out_shape,
        scratch_shapes=[pltpu.VMEM((d // LANES, TOKEN_BLOCK, LANES), F32)],
        compiler_params=pltpu.CompilerParams(
            dimension_semantics=("arbitrary", "arbitrary"), vmem_limit_bytes=VMEM_LIMIT),
        name="in_proj_prompt",
    )(x, gain, w_rm, w_kvt)


def _prompt_bias_tables():
    i = jnp.arange(BLOCK)[:, None]
    j = jnp.arange(2 * BLOCK)[None, :]
    dist = (i + BLOCK) - j
    tabs = []
    slopes_a, slopes_b = _alibi_slopes(A_HEADS), _alibi_slopes(3 * B_HEADS_PER_GROUP)
    valid_a = (dist >= 0) & (dist <= A_WINDOW - 1)
    for h in range(A_HEADS):
        tabs.append(jnp.where(valid_a, -slopes_a[h] * dist.astype(F32), NEG_INF))
    for g, (win, dil) in enumerate(B_GROUPS):
        valid = (dist >= 0) & (dist <= win // dil)
        for hh in range(B_HEADS_PER_GROUP):
            s = slopes_b[g * B_HEADS_PER_GROUP + hh]
            tabs.append(jnp.where(valid, -s * (dist * dil).astype(F32), NEG_INF))
    return jnp.stack(tabs, axis=0).astype(F32)


def _band_unit(q_pairs, kcs, vcs, kps, vps, bias_c, bias_p, sink):
    n = len(q_pairs)
    low = lax.broadcasted_iota(jnp.int32, (BLOCK, 2 * HEAD_DIM), 1) < HEAD_DIM
    ones = jnp.ones((BLOCK, 2 * HEAD_DIM), BF16)
    q2s = []
    for qp in q_pairs:
        zero = jnp.zeros_like(qp)
        q2s.append(jnp.concatenate([jnp.where(low, qp, zero), jnp.where(low, zero, qp)], axis=0))
    nt = lambda a, b: lax.dot_general(a, b, _NT, preferred_element_type=F32)
    rows = lambda a, p: a[2 * BLOCK * p:2 * BLOCK * (p + 1)]
    sc = jnp.concatenate([nt(q2, kc) for q2, kc in zip(q2s, kcs)], axis=0) + bias_c
    if kps is not None:
        sp = jnp.concatenate([nt(q2, kp) for q2, kp in zip(q2s, kps)], axis=0) + bias_p
        m = jnp.max(jnp.maximum(sc, sp), axis=-1, keepdims=True)
    else:
        m = jnp.max(sc, axis=-1, keepdims=True)
    if sink is not None:
        m = jnp.maximum(m, sink[:, 0:1])
    pc = jnp.exp(sc - m).astype(BF16)
    l = jnp.dot(pc, ones, preferred_element_type=F32)
    o = [jnp.dot(rows(pc, p), vcs[p], preferred_element_type=F32) for p in range(n)]
    if kps is not None:
        pp = jnp.exp(sp - m).astype(BF16)
        l = l + jnp.dot(pp, ones, preferred_element_type=F32)
        o = [o[p] + jnp.dot(rows(pp, p), vps[p], preferred_element_type=F32) for p in range(n)]
    if sink is not None:
        l = l + jnp.exp(sink - m)
    o = jnp.concatenate(o, axis=0) / l
    lse = m + jnp.log(l)
    pick = lambda a, p: jnp.where(low, rows(a, p)[:BLOCK], rows(a, p)[BLOCK:])
    return [pick(o, p) for p in range(n)], [pick(lse, p) for p in range(n)]


def _k2p_kernel(sink_ref, bias_ref,
                qa_ref, kva_ref, kvap_ref,
                q1_ref, q1p_ref, q2_ref, q2p_ref, q3_ref,
                oa_ref, og1_ref, og2_ref, og3_ref):
    j = pl.program_id(1)
    pw = 2 * HEAD_DIM
    first_blk = j == 0
    first_sub = (j % 4) == 0

    def biases(units, no_prevs):
        cur = jnp.concatenate([bias_ref[u][:, BLOCK:] for u in units], axis=0)
        prev = jnp.concatenate([jnp.where(f, NEG_INF, bias_ref[u][:, :BLOCK])
                                for u, f in zip(units, no_prevs)], axis=0)
        return cur, prev

    a_sl = lambda ref, part: [ref[0, :, part * 2 * pw + (p // 2) * pw:
                                  part * 2 * pw + (p // 2 + 1) * pw] for p in range(4)]
    bc, bp = biases((0, 1), (first_blk, first_blk))
    o, _ = _band_unit([qa_ref[0, :, p * pw:(p + 1) * pw] for p in range(4)],
                      a_sl(kva_ref, 0), a_sl(kva_ref, 1), a_sl(kvap_ref, 0), a_sl(kvap_ref, 1),
                      bc, bp, jnp.concatenate([sink_ref[0], sink_ref[1]], axis=0))
    oa_ref[0] = jnp.concatenate(o, axis=-1).astype(BF16)

    b_sl = lambda ref, part: [ref[:, part * GROUP_COLS + p * pw:part * GROUP_COLS + (p + 1) * pw]
                              for p in range(2)]
    c1, p1, c2, p2 = q1_ref.at[0], q1p_ref.at[0], q2_ref.at[0, 0], q2p_ref.at[0, 0]
    bc, bp = biases((2, 3), (first_blk, first_sub))
    o, lse = _band_unit(b_sl(c1, 0) + b_sl(c2, 0), b_sl(c1, 1) + b_sl(c2, 1),
                        b_sl(c1, 2) + b_sl(c2, 2), b_sl(p1, 1) + b_sl(p2, 1),
                        b_sl(p1, 2) + b_sl(p2, 2), bc, bp, None)
    og1_ref[0] = jnp.concatenate(o[0:2] + lse[0:2], axis=-1)
    og2_ref[0, 0] = jnp.concatenate(o[2:4] + lse[2:4], axis=-1)
    c3 = q3_ref.at[0, 0]
    o, lse = _band_unit(b_sl(c3, 0), b_sl(c3, 1), b_sl(c3, 2), None, None,
                        bias_ref[4][:, BLOCK:], None, None)
    og3_ref[0, 0] = jnp.concatenate(o + lse, axis=-1)


def _attention_prompt(sinks, qa, kva, qkv1, qkv2, qkv3):
    n, s, _ = qa.shape
    nb = s // BLOCK
    assert nb == 16 and qkv2.shape[2] // BLOCK == 4 and qkv3.shape[2] == BLOCK
    bias = _prompt_bias_tables().reshape(5, 4 * BLOCK, 2 * BLOCK)
    sinks = jnp.broadcast_to(jnp.repeat(sinks.reshape(A_KV_HEADS, 4), BLOCK, axis=1)[:, :, None],
                             (A_KV_HEADS, 4 * BLOCK, 2 * HEAD_DIM)).astype(F32)
    prev = lambda j: jnp.maximum(j - 1, 0)
    in_specs = [
        _const_spec(sinks.shape),
        _const_spec(bias.shape),
        pl.BlockSpec((1, BLOCK, 512), lambda i, j: (i, j, 0)),
        pl.BlockSpec((1, BLOCK, 512), lambda i, j: (i, j, 0)),
        pl.BlockSpec((1, BLOCK, 512), lambda i, j: (i, prev(j), 0)),
        pl.BlockSpec((1, BLOCK, 768), lambda i, j: (i, j, 0)),
        pl.BlockSpec((1, BLOCK, 768), lambda i, j: (i, prev(j), 0)),
        pl.BlockSpec((1, 1, BLOCK, 768), lambda i, j: (i, j // 4, j % 4, 0)),
        pl.BlockSpec((1, 1, BLOCK, 768), lambda i, j: (i, j // 4, prev(j % 4), 0)),
        pl.BlockSpec((1, 1, BLOCK, 768), lambda i, j: (i, j, 0, 0)),
    ]
    out_shape = (
        jax.ShapeDtypeStruct((n, s, 512), BF16),
        jax.ShapeDtypeStruct((n, s, 512), F32),
        jax.ShapeDtypeStruct((n, 4, s // 4, 512), F32),
        jax.ShapeDtypeStruct((n, 16, s // 16, 512), F32),
    )
    out_specs = (
        pl.BlockSpec((1, BLOCK, 512), lambda i, j: (i, j, 0)),
        pl.BlockSpec((1, BLOCK, 512), lambda i, j: (i, j, 0)),
        pl.BlockSpec((1, 1, BLOCK, 512), lambda i, j: (i, j // 4, j % 4, 0)),
        pl.BlockSpec((1, 1, BLOCK, 512), lambda i, j: (i, j, 0, 0)),
    )
    return pl.pallas_call(
        _k2p_kernel,
        grid=(n, nb),
        in_specs=in_specs,
        out_specs=out_specs,
        out_shape=out_shape,
        compiler_params=pltpu.CompilerParams(
            dimension_semantics=("arbitrary", "arbitrary"), vmem_limit_bytes=VMEM_LIMIT),
        name="attention_prompt",
    )(sinks, bias, qa, kva, kva, qkv1, qkv1, qkv2, qkv2, qkv3)


def _sigmoid(x):
    return 1.0 / (1.0 + jnp.exp(-x))


def _mix_tail(oa, ob, gates_ref, x_ref, wba_ref, wbb_ref, wout_ref, gpost_ref, out_ref):
    ya = jnp.dot(oa, wba_ref[...], preferred_element_type=F32)
    yb = jnp.dot(ob, wbb_ref[...], preferred_element_type=F32)
    ga = gates_ref[:, 0:D_MODEL].astype(F32)
    gb = gates_ref[:, D_MODEL:].astype(F32)
    mixed = _sigmoid(ga) * ya + _sigmoid(gb) * yb
    mix = jnp.dot(mixed.astype(BF16), wout_ref[...], preferred_element_type=F32)
    out_ref[...] = x_ref[...] + _rms(mix, gpost_ref[...])


def _k3p_kernel(oa_ref, og1_ref, og2_ref, og3_ref, gates_ref, x_ref,
                wba_ref, wbb_ref, wout_ref, gpost_ref, out_ref, s2_ref, s3_ref):
    n_slabs = 2 * GROUP_COLS // LANES
    for dil, src, dst in ((4, og2_ref, s2_ref), (16, og3_ref, s3_ref)):
        for r in range(dil):
            for s in range(n_slabs):
                dst[s, pl.ds(r, TOKEN_BLOCK // dil, stride=dil), :] = (
                    src[0, r, :, s * LANES:(s + 1) * LANES])
    o1, l1 = og1_ref[0, :, 0:GROUP_COLS], og1_ref[0, :, GROUP_COLS:]
    o2 = jnp.concatenate([s2_ref[0], s2_ref[1]], axis=1)
    l2 = jnp.concatenate([s2_ref[2], s2_ref[3]], axis=1)
    o3 = jnp.concatenate([s3_ref[0], s3_ref[1]], axis=1)
    l3 = jnp.concatenate([s3_ref[2], s3_ref[3]], axis=1)
    m = jnp.maximum(jnp.maximum(l1, l2), l3)
    e1, e2, e3 = jnp.exp(l1 - m), jnp.exp(l2 - m), jnp.exp(l3 - m)
    ob = (e1 * o1 + e2 * o2 + e3 * o3) / (e1 + e2 + e3)
    _mix_tail(oa_ref[0], ob.astype(BF16), gates_ref.at[0], x_ref.at[0],
              wba_ref, wbb_ref, wout_ref, gpost_ref, out_ref.at[0])


def _out_proj_prompt(oa, og1, og2, og3, gates, x, wba, wbb, wout, gpost):
    n, s, d = x.shape
    nb = s // TOKEN_BLOCK
    tok = lambda w: pl.BlockSpec((1, TOKEN_BLOCK, w), lambda i, j: (i, j, 0))
    return pl.pallas_call(
        _k3p_kernel,
        grid=(n, nb),
        in_specs=[
            tok(512), tok(512),
            pl.BlockSpec((1, 4, TOKEN_BLOCK // 4, 512), lambda i, j: (i, 0, j, 0)),
            pl.BlockSpec((1, 16, TOKEN_BLOCK // 16, 512), lambda i, j: (i, 0, j, 0)),
            tok(2 * D_MODEL), tok(d),
            _const_spec(wba.shape), _const_spec(wbb.shape), _const_spec(wout.shape),
            _const_spec(gpost.shape),
        ],
        out_specs=tok(d),
        out_shape=jax.ShapeDtypeStruct((n, s, d), F32),
        scratch_shapes=[pltpu.VMEM((2 * GROUP_COLS // LANES, TOKEN_BLOCK, LANES), F32)] * 2,
        compiler_params=pltpu.CompilerParams(
            dimension_semantics=("arbitrary", "arbitrary"), vmem_limit_bytes=VMEM_LIMIT),
        name="out_proj_prompt",
    )(oa, og1, og2, og3, gates, x, wba, wbb, wout, gpost)


def _k3s_kernel(o_ref, gates_ref, x_ref, wba_ref, wbb_ref, wout_ref, gpost_ref, out_ref):
    oa = o_ref[:, 0:512].astype(BF16)
    ob = o_ref[:, 512:768].astype(BF16)
    _mix_tail(oa, ob, gates_ref, x_ref, wba_ref, wbb_ref, wout_ref, gpost_ref, out_ref)


def _out_proj_sample(o, gates, x, wba, wbb, wout, gpost):
    t, d = x.shape
    full = lambda a: pl.BlockSpec(a.shape, lambda i: (0,) * a.ndim)
    args = (o, gates, x, wba, wbb, wout, gpost)
    return pl.pallas_call(
        _k3s_kernel,
        grid=(1,),
        in_specs=[full(a) for a in args],
        out_specs=pl.BlockSpec((t, d), lambda i: (0, 0)),
        out_shape=jax.ShapeDtypeStruct((t, d), F32),
        compiler_params=pltpu.CompilerParams(
            dimension_semantics=("arbitrary",), vmem_limit_bytes=VMEM_LIMIT),
        name="out_proj_sample",
    )(*args)


FF_CHUNK = 1024


def _gelu_tanh(c):
    return 0.5 * c * (1.0 + jnp.tanh(np.sqrt(2.0 / np.pi).astype(np.float32)
                                     * (c + 0.044715 * (c * c * c))))


def _ffn_chunk(h, w_a, w_g, cw, cb, w_down, prev_fn):
    a = jnp.dot(h, w_a, preferred_element_type=F32)
    g = jnp.dot(h, w_g, preferred_element_type=F32)
    p1, p2 = prev_fn(a)
    conv = cb + cw[0:1] * p2 + cw[1:2] * p1 + cw[2:3] * a
    act = (_gelu_tanh(conv) * g).astype(BF16)
    return jnp.dot(act, w_down, preferred_element_type=F32), a


def _k4p_kernel(x_ref, gpre_ref, gpost_ref, wup_ref, cw_ref, cb_ref, wdown_ref,
                out_ref, tail_ref, hist_ref):
    tb = pl.program_id(1)
    tq = TOKEN_BLOCK

    @pl.when(tb == 0)
    def _():
        hist_ref[:, 0:8, :] = jnp.zeros((hist_ref.shape[0], 8, LANES), F32)

    x = x_ref[0]
    h = _rms(x, gpre_ref[...]).astype(BF16)
    y = jnp.zeros((tq, D_MODEL), F32)
    for c0 in range(0, D_FF, FF_CHUNK):
        c1 = c0 + FF_CHUNK
        slabs = range(c0 // LANES, c1 // LANES)

        def prev(a, c0=c0, slabs=slabs):
            for j in slabs:
                hist_ref[j, 8:8 + tq, :] = a[:, j * LANES - c0:(j + 1) * LANES - c0]
            p1 = jnp.concatenate([hist_ref[j, 7:7 + tq, :] for j in slabs], axis=1)
            p2 = jnp.concatenate([hist_ref[j, 6:6 + tq, :] for j in slabs], axis=1)
            return p1, p2

        yc, a = _ffn_chunk(h, wup_ref[:, c0:c1], wup_ref[:, D_FF + c0:D_FF + c1],
                           cw_ref[:, c0:c1], cb_ref[:, c0:c1], wdown_ref[c0:c1, :], prev)
        tail_ref[0, :, c0:c1] = a[tq - 8:, :]
        for j in slabs:
            hist_ref[j, 0:8, :] = a[tq - 8:, j * LANES - c0:(j + 1) * LANES - c0]
        y = y + yc
    out_ref[0] = x + _rms(y, gpost_ref[...])


def _ffn_prompt(x, gpre, gpost, wup, cw, cb, wdown):
    n, s, d = x.shape
    nb = s // TOKEN_BLOCK
    tok = pl.BlockSpec((1, TOKEN_BLOCK, d), lambda i, j: (i, j, 0))
    consts = (gpre, gpost, wup, cw, cb, wdown)
    return pl.pallas_call(
        _k4p_kernel,
        grid=(n, nb),
        in_specs=[tok] + [_const_spec(a.shape) for a in consts],
        out_specs=(tok, pl.BlockSpec((1, 8, D_FF), lambda i, j: (i, 0, 0))),
        out_shape=(jax.ShapeDtypeStruct((n, s, d), F32), jax.ShapeDtypeStruct((n, 8, D_FF), F32)),
        scratch_shapes=[pltpu.VMEM((D_FF // LANES, 8 + TOKEN_BLOCK, LANES), F32)],
        compiler_params=pltpu.CompilerParams(
            dimension_semantics=("arbitrary", "arbitrary"), vmem_limit_bytes=VMEM_LIMIT),
        name="ffn_prompt",
    )(x, *consts)


def _k4s_kernel(x_ref, e1_ref, e2_ref, gpre_ref, gpost_ref, wa_ref, wg_ref, cw_ref, cb_ref,
                wdown_ref, out_ref, a_ref, h_ref, y_ref):
    c = pl.program_id(0)

    @pl.when(c == 0)
    def _():
        h_ref[...] = _rms(x_ref[...], gpre_ref[...]).astype(BF16)
        y_ref[...] = jnp.zeros_like(y_ref)

    pos = lax.broadcasted_iota(jnp.int32, a_ref.shape, 0) % 4

    def prev(a):
        return (jnp.where(pos == 0, e1_ref[...], pltpu.roll(a, 1, axis=0)),
                jnp.where(pos < 2, e2_ref[...], pltpu.roll(a, 2, axis=0)))

    yc, a = _ffn_chunk(h_ref[...], wa_ref[...], wg_ref[...], cw_ref[...], cb_ref[...],
                       wdown_ref[...], prev)
    a_ref[...] = a
    y_ref[...] += yc

    @pl.when(c == pl.num_programs(0) - 1)
    def _():
        out_ref[...] = x_ref[...] + _rms(y_ref[...], gpost_ref[...])


def _ffn_sample(x, e1, e2, gpre, gpost, wup, cw, cb, wdown):
    t, d = x.shape
    nc = D_FF // FF_CHUNK
    const = lambda a: pl.BlockSpec(a.shape, lambda c: (0,) * a.ndim)
    cols = lambda rows: pl.BlockSpec((rows, FF_CHUNK), lambda c: (0, c))
    return pl.pallas_call(
        _k4s_kernel,
        grid=(nc,),
        in_specs=[const(x), cols(t), cols(t), const(gpre), const(gpost),
                  cols(d), pl.BlockSpec((d, FF_CHUNK), lambda c: (0, nc + c)),
                  cols(3), cols(1), pl.BlockSpec((FF_CHUNK, d), lambda c: (c, 0))],
        out_specs=(pl.BlockSpec((t, d), lambda c: (0, 0)), cols(t)),
        out_shape=(jax.ShapeDtypeStruct((t, d), F32), jax.ShapeDtypeStruct((t, D_FF), F32)),
        scratch_shapes=[pltpu.VMEM((t, d), BF16), pltpu.VMEM((t, d), F32)],
        compiler_params=pltpu.CompilerParams(
            dimension_semantics=("arbitrary",), vmem_limit_bytes=VMEM_LIMIT),
        name="ffn_sample",
    )(x, e1, e2, gpre, gpost, wup, wup, cw, cb, wdown)


def _k1s_kernel(x_ref, g_ref, wrm_ref, wkvt_ref, rep_ref, qbd_ref, kvt_ref, gates_ref):
    h = _rms(x_ref[...], g_ref[...]).astype(BF16)
    shape = (rep_ref.shape[0], GROUP_COLS)
    own = ((lax.broadcasted_iota(jnp.int32, shape, 0) % 16) // 4
           == lax.broadcasted_iota(jnp.int32, shape, 1) // HEAD_DIM)
    starts = (C_QA[0], C_QA[0] + GROUP_COLS, C_QKV1[0], C_QKV2[0], C_QKV3[0])
    for u, c0 in enumerate(starts):
        q = jnp.dot(h, wrm_ref[:, c0:c0 + GROUP_COLS], preferred_element_type=F32)
        q = (q * Q_SCALE).astype(BF16)
        qrep = jnp.dot(rep_ref[...], q, preferred_element_type=F32)
        qbd_ref[:, u * GROUP_COLS:(u + 1) * GROUP_COLS] = jnp.where(own, qrep, 0.0).astype(BF16)
    kvt_ref[...] = lax.dot_general(wkvt_ref[...], h, _NT, preferred_element_type=F32)
    half = (C_GATES[0] + C_GATES[1]) // 2
    gates_ref[:, 0:D_MODEL] = jnp.dot(h, wrm_ref[:, C_GATES[0]:half],
                                      preferred_element_type=F32).astype(BF16)
    gates_ref[:, D_MODEL:] = jnp.dot(h, wrm_ref[:, half:C_GATES[1]],
                                     preferred_element_type=F32).astype(BF16)


def _in_proj_sample(x, gain, w_rm, w_kvt):
    t, d = x.shape
    r = jnp.arange(4 * t)
    src = (r // 16) * 4 + r % 4
    rep = (src[:, None] == jnp.arange(t)[None, :]).astype(BF16)
    args = (x, gain, w_rm, w_kvt, rep)
    full = lambda a: pl.BlockSpec(a.shape, lambda i: (0,) * a.ndim, pipeline_mode=pl.Buffered(1))
    out_shape = (
        jax.ShapeDtypeStruct((4 * t, 1280), BF16),
        jax.ShapeDtypeStruct((w_kvt.shape[0], t), F32),
        jax.ShapeDtypeStruct((t, 2 * D_MODEL), BF16),
    )
    return pl.pallas_call(
        _k1s_kernel,
        grid=(1,),
        in_specs=[full(a) for a in args],
        out_specs=tuple(pl.BlockSpec(s.shape, lambda i: (0, 0)) for s in out_shape),
        out_shape=out_shape,
        compiler_params=pltpu.CompilerParams(
            dimension_semantics=("arbitrary",), vmem_limit_bytes=VMEM_LIMIT),
        name="in_proj_sample",
    )(*args)


_UNIT_COLS = ((0, 256), (256, 512), (512, 768), (768, 1024), (1024, 1280))
_UNIT_W = (A_WINDOW, A_WINDOW, B_GROUPS[0][0], B_GROUPS[1][0], B_GROUPS[2][0])
_UNIT_BIAS_OFF = tuple(int(v) for v in np.cumsum((0,) + _UNIT_W)[:-1])


def _sample_tables(sinks):
    slopes_a, slopes_b = _alibi_slopes(A_HEADS), _alibi_slopes(3 * B_HEADS_PER_GROUP)
    row = jnp.arange(16)
    slot, tok = row // 4, row % 4
    unit_slopes = [slopes_a[slot], slopes_a[4 + slot],
                   slopes_b[slot], slopes_b[4 + slot], slopes_b[8 + slot]]
    dils = (1, 1, 1, 4, 16)
    maxd = (A_WINDOW - 1, A_WINDOW - 1) + tuple(w for w, _ in B_GROUPS)
    bias = []
    for u in range(5):
        w = _UNIT_W[u]
        dist = (w + tok)[:, None] - jnp.arange(w)[None, :]
        valid = (dist % dils[u] == 0) & (dist <= maxd[u])
        bias.append(jnp.where(valid, -unit_slopes[u][:, None] * dist.astype(F32), NEG_INF))
    bias = jnp.concatenate(bias, axis=1).astype(F32)
    slope = jnp.broadcast_to(jnp.stack(unit_slopes)[:, :, None], (5, 16, 128)).astype(F32)
    sink = jnp.stack([sinks[slot], sinks[4 + slot]])
    sink = jnp.broadcast_to(sink[:, :, None], (2, 16, 128)).astype(F32)
    return bias, slope, sink


def _k2s_kernel(qbd_ref, new_ref, ca_ref, cb1_ref, cb2_ref, cb3_ref, bias_ref, slope_ref, sink_ref,
                na_ref, nb1_ref, nb2_ref, nb3_ref, o_ref):
    nl = pl.program_id(0) % SAMPLE_TILE_SEQS
    row = lax.broadcasted_iota(jnp.int32, (16, 128), 0)
    lane = lax.broadcasted_iota(jnp.int32, (16, 128), 1)
    tok, ltok = row % 4, lane % 4
    mine = (lane // 4) == nl
    dist_new = (tok - ltok).astype(F32)
    own = (lax.broadcasted_iota(jnp.int32, (16, GROUP_COLS), 0) // 4
           == lax.broadcasted_iota(jnp.int32, (16, GROUP_COLS), 1) // HEAD_DIM)
    pick = (lax.broadcasted_iota(jnp.int32, (8, 16), 1) % 4
            == lax.broadcasted_iota(jnp.int32, (8, 16), 0)).astype(BF16)

    def scores(u, kt, knew):
        q = qbd_ref[:, _UNIT_COLS[u][0]:_UNIT_COLS[u][1]]
        w = _UNIT_W[u]
        off = _UNIT_BIAS_OFF[u]
        s = jnp.dot(q, kt.astype(BF16), preferred_element_type=F32) + bias_ref[:, off:off + w]
        ok = mine & ((ltok <= tok) if u < 3 else (ltok == tok))
        bn = jnp.where(ok, -slope_ref[u] * dist_new, NEG_INF)
        sn = jnp.dot(q, knew.astype(BF16), preferred_element_type=F32) + bn
        m = jnp.maximum(jnp.max(s, axis=-1, keepdims=True), jnp.max(sn, axis=-1, keepdims=True))
        return s, sn, m

    def weighted(s, sn, m, vt, vnew):
        p, pn = jnp.exp(s - m), jnp.exp(sn - m)
        l = jnp.sum(p, axis=-1, keepdims=True) + jnp.sum(pn, axis=-1, keepdims=True)
        o = (lax.dot_general(p.astype(BF16), vt.astype(BF16), _NT, preferred_element_type=F32)
             + lax.dot_general(pn.astype(BF16), vnew.astype(BF16), _NT, preferred_element_type=F32))
        return o, l

    def finish(o, l):
        o = jnp.where(own, o / l, 0.0).astype(BF16)
        return jnp.dot(pick, o, preferred_element_type=F32)

    tile4 = lambda a: jnp.concatenate([a, a, a, a], axis=0)
    outs = []
    for g in range(A_KV_HEADS):
        kt = tile4(ca_ref[0, g * 64:(g + 1) * 64, :])
        vt = tile4(ca_ref[0, 128 + g * 64:128 + (g + 1) * 64, :])
        knew = tile4(new_ref[R_A[0] + g * 64:R_A[0] + (g + 1) * 64, :])
        vnew = tile4(new_ref[R_A[0] + 128 + g * 64:R_A[0] + 128 + (g + 1) * 64, :])
        s, sn, m = scores(g, kt, knew)
        sink = sink_ref[g][:, 0:1]
        m = jnp.maximum(m, sink)
        o, l = weighted(s, sn, m, vt, vnew)
        outs.append(finish(o, l + jnp.exp(sink - m)))

    parts = []
    for u, (c_ref, rows) in ((2, (cb1_ref, R_B1)), (3, (cb2_ref, R_B2)), (4, (cb3_ref, R_B3))):
        kt, vt = c_ref[0, 0:GROUP_COLS, :], c_ref[0, GROUP_COLS:, :]
        knew = new_ref[rows[0]:rows[0] + GROUP_COLS, :]
        vnew = new_ref[rows[0] + GROUP_COLS:rows[1], :]
        parts.append(scores(u, kt, knew) + (vt, vnew))
    m = functools.reduce(jnp.maximum, [p[2] for p in parts])
    o_sum, l_sum = 0.0, 0.0
    for s, sn, _, vt, vnew in parts:
        o, l = weighted(s, sn, m, vt, vnew)
        o_sum, l_sum = o_sum + o, l_sum + l
    outs.append(finish(o_sum, l_sum))
    o_ref[0] = jnp.concatenate(outs, axis=-1)

    lane_t = lax.broadcasted_iota(jnp.int32, (8, 128), 1)
    for c_ref, n_ref, rows in ((ca_ref, na_ref, R_A), (cb1_ref, nb1_ref, R_B1),
                               (cb2_ref, nb2_ref, R_B2), (cb3_ref, nb3_ref, R_B3)):
        w = c_ref.shape[2]
        nrows = rows[1] - rows[0]
        chunk = 256

        def body(i, carry, c_ref=c_ref, n_ref=n_ref, rows=rows, w=w):
            r0 = pl.multiple_of(i * chunk, chunk)
            x = c_ref[0, pl.ds(r0, chunk), :]
            shifted = pltpu.roll(x, w - 4, axis=1)
            new = new_ref[pl.ds(pl.multiple_of(rows[0] + r0, chunk), chunk), :]
            new = pltpu.roll(new, 124 - 4 * nl, axis=1)
            if w > 128:
                n_ref[0, pl.ds(r0, chunk), 0:w - 128] = shifted[:, 0:w - 128]
            sel = jnp.broadcast_to(lane_t[0:1] >= 124, (chunk, 128))
            n_ref[0, pl.ds(r0, chunk), w - 128:w] = jnp.where(sel, new, shifted[:, w - 128:w])
            return carry

        lax.fori_loop(0, nrows // chunk, body, 0)


def _attention_sample(qbd, kvt_new, ca, cb1, cb2, cb3, sinks):
    nseq = ca.shape[0]
    assert nseq % SAMPLE_TILE_SEQS == 0 and kvt_new.shape[1] == 4 * nseq
    assert (ca.shape[2], cb1.shape[2], cb2.shape[2], cb3.shape[2]) == _UNIT_W[1:]
    bias, slope, sink = _sample_tables(sinks)
    cache_spec = lambda a: pl.BlockSpec((1,) + a.shape[1:], lambda i: (i, 0, 0))
    in_specs = [
        pl.BlockSpec((16, 1280), lambda i: (i, 0)),
        pl.BlockSpec((kvt_new.shape[0], 128), lambda i: (0, i // SAMPLE_TILE_SEQS)),
        cache_spec(ca), cache_spec(cb1), cache_spec(cb2), cache_spec(cb3),
        _const_spec(bias.shape), _const_spec(slope.shape), _const_spec(sink.shape),
    ]
    out_shape = tuple(jax.ShapeDtypeStruct(a.shape, F32) for a in (ca, cb1, cb2, cb3)) + (
        jax.ShapeDtypeStruct((nseq, 8, 768), F32),)
    out_specs = tuple(cache_spec(a) for a in (ca, cb1, cb2, cb3)) + (
        pl.BlockSpec((1, 8, 768), lambda i: (i, 0, 0)),)
    return pl.pallas_call(
        _k2s_kernel,
        grid=(nseq,),
        in_specs=in_specs,
        out_specs=out_specs,
        out_shape=out_shape,
        compiler_params=pltpu.CompilerParams(
            dimension_semantics=("arbitrary",), vmem_limit_bytes=VMEM_LIMIT),
        name="attention_sample",
    )(qbd, kvt_new, ca, cb1, cb2, cb3, bias, slope, sink)


def _to_feature_major(cache):
    n, w = cache.shape[0], cache.shape[1]
    return jnp.transpose(cache, (0, 2, 3, 4, 1)).reshape(n, -1, w)


def _from_feature_major(c, heads):
    n, _, w = c.shape
    return jnp.transpose(c.reshape(n, 2, heads, HEAD_DIM, w), (0, 4, 1, 2, 3))[None]


def kernel(x_prompt, x_sample, cache_a_kv, cache_b1_kv, cache_b2_kv, cache_b3_kv, state_conv,
           w_in, sinks_a, w_branch_a, w_branch_b, w_out, norm_mix_pre, norm_mix_post,
           norm_ffn_pre, norm_ffn_post, w_up, conv_w, conv_b, w_down):
    assert w_in.shape[0] == 1, "single layer"
    w_rm, w_kvt = _prep_in_weights(w_in[0])
    wba, wbb, wout = (w_branch_a[0].astype(BF16), w_branch_b[0].astype(BF16), w_out[0].astype(BF16))
    wup, wdown = w_up[0].astype(BF16), w_down[0].astype(BF16)
    g_pre, g_post, gf_pre, gf_post = norm_mix_pre, norm_mix_post, norm_ffn_pre, norm_ffn_post
    cw, cb = conv_w[0], conv_b

    qa, kva, qkv1, gates, qkv2, qkv3, at, b1t, b2t, b3t = _in_proj_prompt(x_prompt, g_pre, w_rm, w_kvt)
    oa, og1, og2, og3 = _attention_prompt(sinks_a, qa, kva, qkv1, qkv2, qkv3)
    x1 = _out_proj_prompt(oa, og1, og2, og3, gates, x_prompt, wba, wbb, wout, g_post)
    y_prompt, tail = _ffn_prompt(x1, gf_pre, gf_post, wup, cw, cb, wdown)
    new_conv_p = tail[:, 6:8, :][None]

    nseq, tnew, d = x_sample.shape
    assert tnew == 4
    xs = x_sample.reshape(nseq * tnew, d)
    qbd, kvt_new, gates_s = _in_proj_sample(xs, g_pre, w_rm, w_kvt)
    ca, cb1, cb2, cb3 = (_to_feature_major(c[0]) for c in
                         (cache_a_kv, cache_b1_kv, cache_b2_kv, cache_b3_kv))
    na, nb1, nb2, nb3, o_s = _attention_sample(qbd, kvt_new, ca, cb1, cb2, cb3, sinks_a[0])
    o_s = o_s[:, 0:4, :].reshape(nseq * tnew, 768)
    x1s = _out_proj_sample(o_s, gates_s, xs, wba, wbb, wout, g_post)
    st = state_conv[0]
    zero = jnp.zeros_like(st[:, 0])
    e1 = jnp.stack([st[:, 1], zero, zero, zero], axis=1).reshape(nseq * tnew, D_FF)
    e2 = jnp.stack([st[:, 0], st[:, 1], zero, zero], axis=1).reshape(nseq * tnew, D_FF)
    ys, a_s = _ffn_sample(x1s, e1, e2, gf_pre, gf_post, wup, cw, cb, wdown)
    y_sample = ys.reshape(nseq, tnew, d)
    new_conv_s = a_s.reshape(nseq, tnew, D_FF)[:, 2:4, :][None]

    return (y_prompt, y_sample,
            _from_feature_major(at, A_KV_HEADS), _from_feature_major(na, A_KV_HEADS),
            _from_feature_major(b1t, 4), _from_feature_major(nb1, 4),
            _from_feature_major(b2t, 4), _from_feature_major(nb2, 4),
            _from_feature_major(b3t, 4), _from_feature_major(nb3, 4),
            new_conv_p, new_conv_s)
```

```python
import functools

import jax
import jax.numpy as jnp
import numpy as np
from jax import lax
from jax.experimental import pallas as pl
from jax.experimental.pallas import tpu as pltpu

F32 = jnp.float32
BF16 = jnp.bfloat16

HEAD_DIM = 64
A_HEADS = 8
A_KV_HEADS = 2
A_WINDOW = 128
B_GROUPS = ((128, 1), (512, 4), (2048, 16))
B_HEADS_PER_GROUP = 4
D_MODEL = 1024
D_FF = 4096
RMS_EPS = 1e-6
NEG_INF = -1e30
Q_SCALE = HEAD_DIM ** -0.5

LANES = 128
BLOCK = 128
TOKEN_BLOCK = 512
GROUP_COLS = B_HEADS_PER_GROUP * HEAD_DIM
SAMPLE_TILE_SEQS = 32
SAMPLE_STEP_SEQS = 2

C_QA = (0, 512)
C_KVA = (512, 1024)
C_QKV1 = (1024, 1792)
C_GATES = (1792, 3840)
C_QKV2 = (3840, 4608)
C_QKV3 = (4608, 5376)
R_A = (0, 256)
R_B1 = (256, 768)
R_B2 = (768, 1280)
R_B3 = (1280, 1792)

VMEM_LIMIT = 56 * 1024 * 1024

_NT = (((1,), (1,)), ((), ()))


def _const_spec(shape):
    nd = len(shape)
    return pl.BlockSpec(shape, lambda *_: (0,) * nd, pipeline_mode=pl.Buffered(1))


def _rms(x, gain):
    ms = jnp.mean(x * x, axis=-1, keepdims=True)
    return x * lax.rsqrt(ms + RMS_EPS) * gain


def _alibi_slopes(n_heads):
    return jnp.exp2(-8.0 * jnp.arange(1, n_heads + 1, dtype=F32) / n_heads)


def _prep_in_weights(w):
    qa, ka, va = w[:, 0:512], w[:, 512:640], w[:, 640:768]
    qb, kb, vb = w[:, 768:1536], w[:, 1536:2304], w[:, 2304:3072]
    gates = w[:, 3072:5120]
    g = lambda z, i: z[:, GROUP_COLS * i:GROUP_COLS * (i + 1)]
    hd = HEAD_DIM
    kva2 = [z[:, i * hd:(i + 1) * hd] for z in (ka, va) for i in (0, 0, 1, 1)]
    w_rm = jnp.concatenate(
        [qa] + kva2 + [g(qb, 0), g(kb, 0), g(vb, 0), gates,
         g(qb, 1), g(kb, 1), g(vb, 1), g(qb, 2), g(kb, 2), g(vb, 2)], axis=1).astype(BF16)
    w_kvt = jnp.concatenate(
        [ka, va, g(kb, 0), g(vb, 0), g(kb, 1), g(vb, 1), g(kb, 2), g(vb, 2)], axis=1).T.astype(BF16)
    return w_rm, w_kvt


def _store_qkv(ref, r):
    ref[:, 0:GROUP_COLS] = (r[:, 0:GROUP_COLS] * Q_SCALE).astype(BF16)
    ref[:, GROUP_COLS:] = r[:, GROUP_COLS:].astype(BF16)


def _k1p_kernel(x_ref, g_ref, wrm_ref, wkvt_ref,
                qa_ref, kva_ref, qkv1_ref, gates_ref, qkv2_ref, qkv3_ref,
                at_ref, b1t_ref, b2t_ref, b3t_ref, slab_ref):
    tb = pl.program_id(1)
    gain = g_ref[...]

    def proj(hh, c):
        return jnp.dot(hh, wrm_ref[:, c[0]:c[1]], preferred_element_type=F32)

    h = _rms(x_ref[0], gain).astype(BF16)
    qa_ref[0] = (proj(h, C_QA) * Q_SCALE).astype(BF16)
    kva_ref[0] = proj(h, C_KVA).astype(BF16)
    _store_qkv(qkv1_ref.at[0], proj(h, C_QKV1))
    half = (C_GATES[0] + C_GATES[1]) // 2
    gates_ref[0, :, 0:D_MODEL] = proj(h, (C_GATES[0], half)).astype(BF16)
    gates_ref[0, :, D_MODEL:] = proj(h, (half, C_GATES[1])).astype(BF16)

    n_slabs = D_MODEL // LANES
    for s in range(n_slabs):
        slab_ref[s] = x_ref[0, :, s * LANES:(s + 1) * LANES]
    for dil, cols, out_ref in ((4, C_QKV2, qkv2_ref), (16, C_QKV3, qkv3_ref)):
        per = TOKEN_BLOCK // dil
        xs = jnp.concatenate(
            [jnp.concatenate([slab_ref[s, pl.ds(r, per, stride=dil), :] for r in range(dil)], axis=0)
             for s in range(n_slabs)], axis=1)
        hp = _rms(xs, gain).astype(BF16)
        rp = proj(hp, cols)
        out_ref[0, :, :, 0:GROUP_COLS] = (
            (rp[:, 0:GROUP_COLS] * Q_SCALE).astype(BF16).reshape(dil, per, GROUP_COLS))
        out_ref[0, :, :, GROUP_COLS:] = (
            rp[:, GROUP_COLS:].astype(BF16).reshape(dil, per, 2 * GROUP_COLS))

    b3t_ref[0] = lax.dot_general(wkvt_ref[R_B3[0]:R_B3[1], :], h, _NT, preferred_element_type=F32)

    @pl.when(tb == pl.num_programs(1) - 1)
    def _():
        b2t_ref[0] = lax.dot_general(wkvt_ref[R_B2[0]:R_B2[1], :], h, _NT, preferred_element_type=F32)
        t = lax.dot_general(wkvt_ref[R_A[0]:R_B1[1], :], h[TOKEN_BLOCK - BLOCK:, :], _NT,
                            preferred_element_type=F32)
        at_ref[0] = t[R_A[0]:R_A[1]]
        b1t_ref[0] = t[R_B1[0]:R_B1[1]]


def _in_proj_prompt(x, gain, w_rm, w_kvt):
    n, s, d = x.shape
    assert s % TOKEN_BLOCK == 0 and TOKEN_BLOCK == B_GROUPS[1][0] and s == B_GROUPS[2][0]
    nb = s // TOKEN_BLOCK
    out_shape = (
        jax.ShapeDtypeStruct((n, s, 512), BF16),
        jax.ShapeDtypeStruct((n, s, 512), BF16),
        jax.ShapeDtypeStruct((n, s, 768), BF16),
        jax.ShapeDtypeStruct((n, s, 2 * D_MODEL), BF16),
        jax.ShapeDtypeStruct((n, 4, s // 4, 768), BF16),
        jax.ShapeDtypeStruct((n, 16, s // 16, 768), BF16),
        jax.ShapeDtypeStruct((n, 256, BLOCK), F32),
        jax.ShapeDtypeStruct((n, 512, BLOCK), F32),
        jax.ShapeDtypeStruct((n, 512, TOKEN_BLOCK), F32),
        jax.ShapeDtypeStruct((n, 512, s), F32),
    )
    tok = lambda w: pl.BlockSpec((1, TOKEN_BLOCK, w), lambda i, j: (i, j, 0))
    out_specs = (
        tok(512), tok(512), tok(768), tok(2 * D_MODEL),
        pl.BlockSpec((1, 4, TOKEN_BLOCK // 4, 768), lambda i, j: (i, 0, j, 0)),
        pl.BlockSpec((1, 16, TOKEN_BLOCK // 16, 768), lambda i, j: (i, 0, j, 0)),
        pl.BlockSpec((1, 256, BLOCK), lambda i, j: (i, 0, 0)),
        pl.BlockSpec((1, 512, BLOCK), lambda i, j: (i, 0, 0)),
        pl.BlockSpec((1, 512, TOKEN_BLOCK), lambda i, j: (i, 0, 0)),
        pl.BlockSpec((1, 512, TOKEN_BLOCK), lambda i, j: (i, 0, j)),
    )
    return pl.pallas_call(
        _k1p_kernel,
        grid=(n, nb),
        in_specs=[tok(d), _const_spec((1, d)), _const_spec(w_rm.shape), _const_spec(w_kvt.shape)],
        out_specs=out_specs,
        out_shape=out_shape,
        scratch_shapes=[pltpu.VMEM((d // LANES, TOKEN_BLOCK, LANES), F32)],
        compiler_params=pltpu.CompilerParams(
            dimension_semantics=("arbitrary", "arbitrary"), vmem_limit_bytes=VMEM_LIMIT),
        name="in_proj_prompt",
    )(x, gain, w_rm, w_kvt)


def _prompt_bias_tables():
    i = jnp.arange(BLOCK)[:, None]
    j = jnp.arange(2 * BLOCK)[None, :]
    dist = (i + BLOCK) - j
    tabs = []
    slopes_a, slopes_b = _alibi_slopes(A_HEADS), _alibi_slopes(3 * B_HEADS_PER_GROUP)
    valid_a = (dist >= 0) & (dist <= A_WINDOW - 1)
    for h in range(A_HEADS):
        tabs.append(jnp.where(valid_a, -slopes_a[h] * dist.astype(F32), NEG_INF))
    for g, (win, dil) in enumerate(B_GROUPS):
        valid = (dist >= 0) & (dist <= win // dil)
        for hh in range(B_HEADS_PER_GROUP):
            s = slopes_b[g * B_HEADS_PER_GROUP + hh]
            tabs.append(jnp.where(valid, -s * (dist * dil).astype(F32), NEG_INF))
    return jnp.stack(tabs, axis=0).astype(F32)


def _band_unit(q_pairs, kcs, vcs, kps, vps, bias_c, bias_p, sink):
    n = len(q_pairs)
    low = lax.broadcasted_iota(jnp.int32, (BLOCK, 2 * HEAD_DIM), 1) < HEAD_DIM
    ones = jnp.ones((BLOCK, 2 * HEAD_DIM), BF16)
    q2s = []
    for qp in q_pairs:
        zero = jnp.zeros_like(qp)
        q2s.append(jnp.concatenate([jnp.where(low, qp, zero), jnp.where(low, zero, qp)], axis=0))
    nt = lambda a, b: lax.dot_general(a, b, _NT, preferred_element_type=F32)
    rows = lambda a, p: a[2 * BLOCK * p:2 * BLOCK * (p + 1)]
    sc = jnp.concatenate([nt(q2, kc) for q2, kc in zip(q2s, kcs)], axis=0) + bias_c
    if kps is not None:
        sp = jnp.concatenate([nt(q2, kp) for q2, kp in zip(q2s, kps)], axis=0) + bias_p
        m = jnp.max(jnp.maximum(sc, sp), axis=-1, keepdims=True)
    else:
        m = jnp.max(sc, axis=-1, keepdims=True)
    if sink is not None:
        m = jnp.maximum(m, sink[:, 0:1])
    pc = jnp.exp(sc - m).astype(BF16)
    l = jnp.dot(pc, ones, preferred_element_type=F32)
    o = [jnp.dot(rows(pc, p), vcs[p], preferred_element_type=F32) for p in range(n)]
    if kps is not None:
        pp = jnp.exp(sp - m).astype(BF16)
        l = l + jnp.dot(pp, ones, preferred_element_type=F32)
        o = [o[p] + jnp.dot(rows(pp, p), vps[p], preferred_element_type=F32) for p in range(n)]
    if sink is not None:
        l = l + jnp.exp(sink - m)
    o = jnp.concatenate(o, axis=0) / l
    lse = m + jnp.log(l)
    pick = lambda a, p: jnp.where(low, rows(a, p)[:BLOCK], rows(a, p)[BLOCK:])
    return [pick(o, p) for p in range(n)], [pick(lse, p) for p in range(n)]


def _k2p_kernel(sink_ref, bias_ref,
                qa_ref, kva_ref, kvap_ref,
                q1_ref, q1p_ref, q2_ref, q2p_ref, q3_ref,
                oa_ref, og1_ref, og2_ref, og3_ref):
    j = pl.program_id(1)
    pw = 2 * HEAD_DIM
    first_blk = j == 0
    first_sub = (j % 4) == 0

    def biases(units, no_prevs):
        cur = jnp.concatenate([bias_ref[u][:, BLOCK:] for u in units], axis=0)
        prev = jnp.concatenate([jnp.where(f, NEG_INF, bias_ref[u][:, :BLOCK])
                                for u, f in zip(units, no_prevs)], axis=0)
        return cur, prev

    a_sl = lambda ref, part: [ref[0, :, part * 2 * pw + (p // 2) * pw:
                                  part * 2 * pw + (p // 2 + 1) * pw] for p in range(4)]
    bc, bp = biases((0, 1), (first_blk, first_blk))
    o, _ = _band_unit([qa_ref[0, :, p * pw:(p + 1) * pw] for p in range(4)],
                      a_sl(kva_ref, 0), a_sl(kva_ref, 1), a_sl(kvap_ref, 0), a_sl(kvap_ref, 1),
                      bc, bp, jnp.concatenate([sink_ref[0], sink_ref[1]], axis=0))
    oa_ref[0] = jnp.concatenate(o, axis=-1).astype(BF16)

    b_sl = lambda ref, part: [ref[:, part * GROUP_COLS + p * pw:part * GROUP_COLS + (p + 1) * pw]
                              for p in range(2)]
    c1, p1, c2, p2 = q1_ref.at[0], q1p_ref.at[0], q2_ref.at[0, 0], q2p_ref.at[0, 0]
    bc, bp = biases((2, 3), (first_blk, first_sub))
    o, lse = _band_unit(b_sl(c1, 0) + b_sl(c2, 0), b_sl(c1, 1) + b_sl(c2, 1),
                        b_sl(c1, 2) + b_sl(c2, 2), b_sl(p1, 1) + b_sl(p2, 1),
                        b_sl(p1, 2) + b_sl(p2, 2), bc, bp, None)
    og1_ref[0] = jnp.concatenate(o[0:2] + lse[0:2], axis=-1)
    og2_ref[0, 0] = jnp.concatenate(o[2:4] + lse[2:4], axis=-1)
    c3 = q3_ref.at[0, 0]
    o, lse = _band_unit(b_sl(c3, 0), b_sl(c3, 1), b_sl(c3, 2), None, None,
                        bias_ref[4][:, BLOCK:], None, None)
    og3_ref[0, 0] = jnp.concatenate(o + lse, axis=-1)


def _attention_prompt(sinks, qa, kva, qkv1, qkv2, qkv3):
    n, s, _ = qa.shape
    nb = s // BLOCK
    assert nb == 16 and qkv2.shape[2] // BLOCK == 4 and qkv3.shape[2] == BLOCK
    bias = _prompt_bias_tables().reshape(5, 4 * BLOCK, 2 * BLOCK)
    sinks = jnp.broadcast_to(jnp.repeat(sinks.reshape(A_KV_HEADS, 4), BLOCK, axis=1)[:, :, None],
                             (A_KV_HEADS, 4 * BLOCK, 2 * HEAD_DIM)).astype(F32)
    prev = lambda j: jnp.maximum(j - 1, 0)
    in_specs = [
        _const_spec(sinks.shape),
        _const_spec(bias.shape),
        pl.BlockSpec((1, BLOCK, 512), lambda i, j: (i, j, 0)),
        pl.BlockSpec((1, BLOCK, 512), lambda i, j: (i, j, 0)),
        pl.BlockSpec((1, BLOCK, 512), lambda i, j: (i, prev(j), 0)),
        pl.BlockSpec((1, BLOCK, 768), lambda i, j: (i, j, 0)),
        pl.BlockSpec((1, BLOCK, 768), lambda i, j: (i, prev(j), 0)),
        pl.BlockSpec((1, 1, BLOCK, 768), lambda i, j: (i, j // 4, j % 4, 0)),
        pl.BlockSpec((1, 1, BLOCK, 768), lambda i, j: (i, j // 4, prev(j % 4), 0)),
        pl.BlockSpec((1, 1, BLOCK, 768), lambda i, j: (i, j, 0, 0)),
    ]
    out_shape = (
        jax.ShapeDtypeStruct((n, s, 512), BF16),
        jax.ShapeDtypeStruct((n, s, 512), F32),
        jax.ShapeDtypeStruct((n, 4, s // 4, 512), F32),
        jax.ShapeDtypeStruct((n, 16, s // 16, 512), F32),
    )
    out_specs = (
        pl.BlockSpec((1, BLOCK, 512), lambda i, j: (i, j, 0)),
        pl.BlockSpec((1, BLOCK, 512), lambda i, j: (i, j, 0)),
        pl.BlockSpec((1, 1, BLOCK, 512), lambda i, j: (i, j // 4, j % 4, 0)),
        pl.BlockSpec((1, 1, BLOCK, 512), lambda i, j: (i, j, 0, 0)),
    )
    return pl.pallas_call(
        _k2p_kernel,
        grid=(n, nb),
        in_specs=in_specs,
        out_specs=out_specs,
        out_shape=out_shape,
        compiler_params=pltpu.CompilerParams(
            dimension_semantics=("arbitrary", "arbitrary"), vmem_limit_bytes=VMEM_LIMIT),
        name="attention_prompt",
    )(sinks, bias, qa, kva, kva, qkv1, qkv1, qkv2, qkv2, qkv3)


def _sigmoid(x):
    return 1.0 / (1.0 + jnp.exp(-x))


def _mix_tail(oa, ob, gates_ref, x_ref, wba_ref, wbb_ref, wout_ref, gpost_ref, out_ref):
    ya = jnp.dot(oa, wba_ref[...], preferred_element_type=F32)
    yb = jnp.dot(ob, wbb_ref[...], preferred_element_type=F32)
    ga = gates_ref[:, 0:D_MODEL].astype(F32)
    gb = gates_ref[:, D_MODEL:].astype(F32)
    mixed = _sigmoid(ga) * ya + _sigmoid(gb) * yb
    mix = jnp.dot(mixed.astype(BF16), wout_ref[...], preferred_element_type=F32)
    out_ref[...] = x_ref[...] + _rms(mix, gpost_ref[...])


def _k3p_kernel(oa_ref, og1_ref, og2_ref, og3_ref, gates_ref, x_ref,
                wba_ref, wbb_ref, wout_ref, gpost_ref, out_ref, s2_ref, s3_ref):
    n_slabs = 2 * GROUP_COLS // LANES
    for dil, src, dst in ((4, og2_ref, s2_ref), (16, og3_ref, s3_ref)):
        for r in range(dil):
            for s in range(n_slabs):
                dst[s, pl.ds(r, TOKEN_BLOCK // dil, stride=dil), :] = (
                    src[0, r, :, s * LANES:(s + 1) * LANES])
    o1, l1 = og1_ref[0, :, 0:GROUP_COLS], og1_ref[0, :, GROUP_COLS:]
    o2 = jnp.concatenate([s2_ref[0], s2_ref[1]], axis=1)
    l2 = jnp.concatenate([s2_ref[2], s2_ref[3]], axis=1)
    o3 = jnp.concatenate([s3_ref[0], s3_ref[1]], axis=1)
    l3 = jnp.concatenate([s3_ref[2], s3_ref[3]], axis=1)
    m = jnp.maximum(jnp.maximum(l1, l2), l3)
    e1, e2, e3 = jnp.exp(l1 - m), jnp.exp(l2 - m), jnp.exp(l3 - m)
    ob = (e1 * o1 + e2 * o2 + e3 * o3) / (e1 + e2 + e3)
    _mix_tail(oa_ref[0], ob.astype(BF16), gates_ref.at[0], x_ref.at[0],
              wba_ref, wbb_ref, wout_ref, gpost_ref, out_ref.at[0])


def _out_proj_prompt(oa, og1, og2, og3, gates, x, wba, wbb, wout, gpost):
    n, s, d = x.shape
    nb = s // TOKEN_BLOCK
    tok = lambda w: pl.BlockSpec((1, TOKEN_BLOCK, w), lambda i, j: (i, j, 0))
    return pl.pallas_call(
        _k3p_kernel,
        grid=(n, nb),
        in_specs=[
            tok(512), tok(512),
            pl.BlockSpec((1, 4, TOKEN_BLOCK // 4, 512), lambda i, j: (i, 0, j, 0)),
            pl.BlockSpec((1, 16, TOKEN_BLOCK // 16, 512), lambda i, j: (i, 0, j, 0)),
            tok(2 * D_MODEL), tok(d),
            _const_spec(wba.shape), _const_spec(wbb.shape), _const_spec(wout.shape),
            _const_spec(gpost.shape),
        ],
        out_specs=tok(d),
        out_shape=jax.ShapeDtypeStruct((n, s, d), F32),
        scratch_shapes=[pltpu.VMEM((2 * GROUP_COLS // LANES, TOKEN_BLOCK, LANES), F32)] * 2,
        compiler_params=pltpu.CompilerParams(
            dimension_semantics=("arbitrary", "arbitrary"), vmem_limit_bytes=VMEM_LIMIT),
        name="out_proj_prompt",
    )(oa, og1, og2, og3, gates, x, wba, wbb, wout, gpost)


def _k3s_kernel(o_ref, gates_ref, x_ref, wba_ref, wbb_ref, wout_ref, gpost_ref, out_ref):
    oa = o_ref[:, 0:512].astype(BF16)
    ob = o_ref[:, 512:768].astype(BF16)
    _mix_tail(oa, ob, gates_ref, x_ref, wba_ref, wbb_ref, wout_ref, gpost_ref, out_ref)


def _out_proj_sample(o, gates, x, wba, wbb, wout, gpost):
    t, d = x.shape
    full = lambda a: pl.BlockSpec(a.shape, lambda i: (0,) * a.ndim)
    args = (o, gates, x, wba, wbb, wout, gpost)
    return pl.pallas_call(
        _k3s_kernel,
        grid=(1,),
        in_specs=[full(a) for a in args],
        out_specs=pl.BlockSpec((t, d), lambda i: (0, 0)),
        out_shape=jax.ShapeDtypeStruct((t, d), F32),
        compiler_params=pltpu.CompilerParams(
            dimension_semantics=("arbitrary",), vmem_limit_bytes=VMEM_LIMIT),
        name="out_proj_sample",
    )(*args)


FF_CHUNK = 1024


def _gelu_tanh(c):
    return 0.5 * c * (1.0 + jnp.tanh(np.sqrt(2.0 / np.pi).astype(np.float32)
                                     * (c + 0.044715 * (c * c * c))))


def _ffn_chunk(h, w_a, w_g, cw, cb, w_down, prev_fn):
    a = jnp.dot(h, w_a, preferred_element_type=F32)
    g = jnp.dot(h, w_g, preferred_element_type=F32)
    p1, p2 = prev_fn(a)
    conv = cb + cw[0:1] * p2 + cw[1:2] * p1 + cw[2:3] * a
    act = (_gelu_tanh(conv) * g).astype(BF16)
    return jnp.dot(act, w_down, preferred_element_type=F32), a


def _k4p_kernel(x_ref, gpre_ref, gpost_ref, wup_ref, cw_ref, cb_ref, wdown_ref,
                out_ref, tail_ref, hist_ref):
    tb = pl.program_id(1)
    tq = TOKEN_BLOCK

    @pl.when(tb == 0)
    def _():
        hist_ref[:, 0:8, :] = jnp.zeros((hist_ref.shape[0], 8, LANES), F32)

    x = x_ref[0]
    h = _rms(x, gpre_ref[...]).astype(BF16)
    y = jnp.zeros((tq, D_MODEL), F32)
    for c0 in range(0, D_FF, FF_CHUNK):
        c1 = c0 + FF_CHUNK
        slabs = range(c0 // LANES, c1 // LANES)

        def prev(a, c0=c0, slabs=slabs):
            for j in slabs:
                hist_ref[j, 8:8 + tq, :] = a[:, j * LANES - c0:(j + 1) * LANES - c0]
            p1 = jnp.concatenate([hist_ref[j, 7:7 + tq, :] for j in slabs], axis=1)
            p2 = jnp.concatenate([hist_ref[j, 6:6 + tq, :] for j in slabs], axis=1)
            return p1, p2

        yc, a = _ffn_chunk(h, wup_ref[:, c0:c1], wup_ref[:, D_FF + c0:D_FF + c1],
                           cw_ref[:, c0:c1], cb_ref[:, c0:c1], wdown_ref[c0:c1, :], prev)
        tail_ref[0, :, c0:c1] = a[tq - 8:, :]
        for j in slabs:
            hist_ref[j, 0:8, :] = a[tq - 8:, j * LANES - c0:(j + 1) * LANES - c0]
        y = y + yc
    out_ref[0] = x + _rms(y, gpost_ref[...])


def _ffn_prompt(x, gpre, gpost, wup, cw, cb, wdown):
    n, s, d = x.shape
    nb = s // TOKEN_BLOCK
    tok = pl.BlockSpec((1, TOKEN_BLOCK, d), lambda i, j: (i, j, 0))
    consts = (gpre, gpost, wup, cw, cb, wdown)
    return pl.pallas_call(
        _k4p_kernel,
        grid=(n, nb),
        in_specs=[tok] + [_const_spec(a.shape) for a in consts],
        out_specs=(tok, pl.BlockSpec((1, 8, D_FF), lambda i, j: (i, 0, 0))),
        out_shape=(jax.ShapeDtypeStruct((n, s, d), F32), jax.ShapeDtypeStruct((n, 8, D_FF), F32)),
        scratch_shapes=[pltpu.VMEM((D_FF // LANES, 8 + TOKEN_BLOCK, LANES), F32)],
        compiler_params=pltpu.CompilerParams(
            dimension_semantics=("arbitrary", "arbitrary"), vmem_limit_bytes=VMEM_LIMIT),
        name="ffn_prompt",
    )(x, *consts)


def _k4s_kernel(x_ref, e1_ref, e2_ref, gpre_ref, gpost_ref, wa_ref, wg_ref, cw_ref, cb_ref,
                wdown_ref, out_ref, a_ref, h_ref, y_ref):
    c = pl.program_id(0)

    @pl.when(c == 0)
    def _():
        h_ref[...] = _rms(x_ref[...], gpre_ref[...]).astype(BF16)
        y_ref[...] = jnp.zeros_like(y_ref)

    pos = lax.broadcasted_iota(jnp.int32, a_ref.shape, 0) % 4

    def prev(a):
        return (jnp.where(pos == 0, e1_ref[...], pltpu.roll(a, 1, axis=0)),
                jnp.where(pos < 2, e2_ref[...], pltpu.roll(a, 2, axis=0)))

    yc, a = _ffn_chunk(h_ref[...], wa_ref[...], wg_ref[...], cw_ref[...], cb_ref[...],
                       wdown_ref[...], prev)
    a_ref[...] = a
    y_ref[...] += yc

    @pl.when(c == pl.num_programs(0) - 1)
    def _():
        out_ref[...] = x_ref[...] + _rms(y_ref[...], gpost_ref[...])


def _ffn_sample(x, e1, e2, gpre, gpost, wup, cw, cb, wdown):
    t, d = x.shape
    nc = D_FF // FF_CHUNK
    const = lambda a: pl.BlockSpec(a.shape, lambda c: (0,) * a.ndim)
    cols = lambda rows: pl.BlockSpec((rows, FF_CHUNK), lambda c: (0, c))
    return pl.pallas_call(
        _k4s_kernel,
        grid=(nc,),
        in_specs=[const(x), cols(t), cols(t), const(gpre), const(gpost),
                  cols(d), pl.BlockSpec((d, FF_CHUNK), lambda c: (0, nc + c)),
                  cols(3), cols(1), pl.BlockSpec((FF_CHUNK, d), lambda c: (c, 0))],
        out_specs=(pl.BlockSpec((t, d), lambda c: (0, 0)), cols(t)),
        out_shape=(jax.ShapeDtypeStruct((t, d), F32), jax.ShapeDtypeStruct((t, D_FF), F32)),
        scratch_shapes=[pltpu.VMEM((t, d), BF16), pltpu.VMEM((t, d), F32)],
        compiler_params=pltpu.CompilerParams(
            dimension_semantics=("arbitrary",), vmem_limit_bytes=VMEM_LIMIT),
        name="ffn_sample",
    )(x, e1, e2, gpre, gpost, wup, wup, cw, cb, wdown)


def _k1s_kernel(x_ref, g_ref, wrm_ref, wkvt_ref, rep_ref, qbd_ref, kvt_ref, gates_ref):
    h = _rms(x_ref[...], g_ref[...]).astype(BF16)
    shape = (rep_ref.shape[0], GROUP_COLS)
    own = ((lax.broadcasted_iota(jnp.int32, shape, 0) % 16) // 4
           == lax.broadcasted_iota(jnp.int32, shape, 1) // HEAD_DIM)
    starts = (C_QA[0], C_QA[0] + GROUP_COLS, C_QKV1[0], C_QKV2[0], C_QKV3[0])
    for u, c0 in enumerate(starts):
        q = jnp.dot(h, wrm_ref[:, c0:c0 + GROUP_COLS], preferred_element_type=F32)
        q = (q * Q_SCALE).astype(BF16)
        qrep = jnp.dot(rep_ref[...], q, preferred_element_type=F32)
        qbd_ref[:, u * GROUP_COLS:(u + 1) * GROUP_COLS] = jnp.where(own, qrep, 0.0).astype(BF16)
    kvt_ref[...] = lax.dot_general(wkvt_ref[...], h, _NT, preferred_element_type=F32)
    half = (C_GATES[0] + C_GATES[1]) // 2
    gates_ref[:, 0:D_MODEL] = jnp.dot(h, wrm_ref[:, C_GATES[0]:half],
                                      preferred_element_type=F32).astype(BF16)
    gates_ref[:, D_MODEL:] = jnp.dot(h, wrm_ref[:, half:C_GATES[1]],
                                     preferred_element_type=F32).astype(BF16)


def _in_proj_sample(x, gain, w_rm, w_kvt):
    t, d = x.shape
    r = jnp.arange(4 * t)
    src = (r // 16) * 4 + r % 4
    rep = (src[:, None] == jnp.arange(t)[None, :]).astype(BF16)
    args = (x, gain, w_rm, w_kvt, rep)
    full = lambda a: pl.BlockSpec(a.shape, lambda i: (0,) * a.ndim, pipeline_mode=pl.Buffered(1))
    out_shape = (
        jax.ShapeDtypeStruct((4 * t, 1280), BF16),
        jax.ShapeDtypeStruct((w_kvt.shape[0], t), F32),
        jax.ShapeDtypeStruct((t, 2 * D_MODEL), BF16),
    )
    return pl.pallas_call(
        _k1s_kernel,
        grid=(1,),
        in_specs=[full(a) for a in args],
        out_specs=tuple(pl.BlockSpec(s.shape, lambda i: (0, 0)) for s in out_shape),
        out_shape=out_shape,
        compiler_params=pltpu.CompilerParams(
            dimension_semantics=("arbitrary",), vmem_limit_bytes=VMEM_LIMIT),
        name="in_proj_sample",
    )(*args)


_UNIT_COLS = ((0, 256), (256, 512), (512, 768), (768, 1024), (1024, 1280))
_UNIT_W = (A_WINDOW, A_WINDOW, B_GROUPS[0][0], B_GROUPS[1][0], B_GROUPS[2][0])
_UNIT_BIAS_OFF = tuple(int(v) for v in np.cumsum((0,) + _UNIT_W)[:-1])


def _sample_tables(sinks):
    slopes_a, slopes_b = _alibi_slopes(A_HEADS), _alibi_slopes(3 * B_HEADS_PER_GROUP)
    row = jnp.arange(16)
    slot, tok = row // 4, row % 4
    unit_slopes = [slopes_a[slot], slopes_a[4 + slot],
                   slopes_b[slot], slopes_b[4 + slot], slopes_b[8 + slot]]
    dils = (1, 1, 1, 4, 16)
    maxd = (A_WINDOW - 1, A_WINDOW - 1) + tuple(w for w, _ in B_GROUPS)
    bias = []
    for u in range(5):
        w = _UNIT_W[u]
        dist = (w + tok)[:, None] - jnp.arange(w)[None, :]
        valid = (dist % dils[u] == 0) & (dist <= maxd[u])
        bias.append(jnp.where(valid, -unit_slopes[u][:, None] * dist.astype(F32), NEG_INF))
    bias = jnp.concatenate(bias, axis=1).astype(F32)
    slope = jnp.broadcast_to(jnp.stack(unit_slopes)[:, :, None], (5, 16, 128)).astype(F32)
    sink = jnp.stack([sinks[slot], sinks[4 + slot]])
    sink = jnp.broadcast_to(sink[:, :, None], (2, 16, 128)).astype(F32)
    return bias, slope, sink


def _k2s_kernel(qbd_ref, new_ref, ca_ref, cb1_ref, cb2_ref, cb3_ref, bias_ref, slope_ref, sink_ref,
                na_ref, nb1_ref, nb2_ref, nb3_ref, o_ref):
    ns = SAMPLE_STEP_SEQS
    nls = [(pl.program_id(0) * ns + s) % SAMPLE_TILE_SEQS for s in range(ns)]
    row = lax.broadcasted_iota(jnp.int32, (16, 128), 0)
    lane = lax.broadcasted_iota(jnp.int32, (16, 128), 1)
    tok, ltok = row % 4, lane % 4
    mines = [(lane // 4) == nl for nl in nls]
    dist_new = (tok - ltok).astype(F32)
    own = (lax.broadcasted_iota(jnp.int32, (16, GROUP_COLS), 0) // 4
           == lax.broadcasted_iota(jnp.int32, (16, GROUP_COLS), 1) // HEAD_DIM)
    pick = (lax.broadcasted_iota(jnp.int32, (8, 16), 1) % 4
            == lax.broadcasted_iota(jnp.int32, (8, 16), 0)).astype(BF16)
    seq_rows = lambda a, s: a[16 * s:16 * (s + 1)]

    def scores(u, kts, knew):
        w = _UNIT_W[u]
        off = _UNIT_BIAS_OFF[u]
        kn = knew.astype(BF16)
        ss, sns = [], []
        for s in range(ns):
            q = qbd_ref[16 * s:16 * (s + 1), _UNIT_COLS[u][0]:_UNIT_COLS[u][1]]
            ss.append(jnp.dot(q, kts[s].astype(BF16), preferred_element_type=F32)
                      + bias_ref[:, off:off + w])
            ok = mines[s] & ((ltok <= tok) if u < 3 else (ltok == tok))
            bn = jnp.where(ok, -slope_ref[u] * dist_new, NEG_INF)
            sns.append(jnp.dot(q, kn, preferred_element_type=F32) + bn)
        sc, sn = jnp.concatenate(ss, axis=0), jnp.concatenate(sns, axis=0)
        m = jnp.maximum(jnp.max(sc, axis=-1, keepdims=True), jnp.max(sn, axis=-1, keepdims=True))
        return sc, sn, m

    def weighted(sc, sn, m, vts, vnew):
        p, pn = jnp.exp(sc - m), jnp.exp(sn - m)
        l = jnp.sum(p, axis=-1, keepdims=True) + jnp.sum(pn, axis=-1, keepdims=True)
        pb, pnb, vn = p.astype(BF16), pn.astype(BF16), vnew.astype(BF16)
        o = jnp.concatenate(
            [lax.dot_general(seq_rows(pb, s), vts[s].astype(BF16), _NT, preferred_element_type=F32)
             + lax.dot_general(seq_rows(pnb, s), vn, _NT, preferred_element_type=F32)
             for s in range(ns)], axis=0)
        return o, l

    def finish(o, l):
        o = o / l
        return [jnp.dot(pick, jnp.where(own, seq_rows(o, s), 0.0).astype(BF16),
                        preferred_element_type=F32) for s in range(ns)]

    tile4 = lambda a: jnp.concatenate([a, a, a, a], axis=0)
    outs = []
    for g in range(A_KV_HEADS):
        kts = [tile4(ca_ref[s, g * 64:(g + 1) * 64, :]) for s in range(ns)]
        vts = [tile4(ca_ref[s, 128 + g * 64:128 + (g + 1) * 64, :]) for s in range(ns)]
        knew = tile4(new_ref[R_A[0] + g * 64:R_A[0] + (g + 1) * 64, :])
        vnew = tile4(new_ref[R_A[0] + 128 + g * 64:R_A[0] + 128 + (g + 1) * 64, :])
        sc, sn, m = scores(g, kts, knew)
        sink = jnp.concatenate([sink_ref[g][:, 0:1]] * ns, axis=0)
        m = jnp.maximum(m, sink)
        o, l = weighted(sc, sn, m, vts, vnew)
        outs.append(finish(o, l + jnp.exp(sink - m)))

    parts = []
    for u, (c_ref, rows) in ((2, (cb1_ref, R_B1)), (3, (cb2_ref, R_B2)), (4, (cb3_ref, R_B3))):
        kts = [c_ref[s, 0:GROUP_COLS, :] for s in range(ns)]
        vts = [c_ref[s, GROUP_COLS:, :] for s in range(ns)]
        knew = new_ref[rows[0]:rows[0] + GROUP_COLS, :]
        vnew = new_ref[rows[0] + GROUP_COLS:rows[1], :]
        parts.append(scores(u, kts, knew) + (vts, vnew))
    m = functools.reduce(jnp.maximum, [p[2] for p in parts])
    o_sum, l_sum = 0.0, 0.0
    for sc, sn, _, vts, vnew in parts:
        o, l = weighted(sc, sn, m, vts, vnew)
        o_sum, l_sum = o_sum + o, l_sum + l
    outs.append(finish(o_sum, l_sum))
    for s in range(ns):
        o_ref[s] = jnp.concatenate([u_out[s] for u_out in outs], axis=-1)

    lane_t = lax.broadcasted_iota(jnp.int32, (8, 128), 1)
    for s in range(ns):
        for c_ref, n_ref, rows in ((ca_ref, na_ref, R_A), (cb1_ref, nb1_ref, R_B1),
                                   (cb2_ref, nb2_ref, R_B2), (cb3_ref, nb3_ref, R_B3)):
            w = c_ref.shape[2]
            nrows = rows[1] - rows[0]
            chunk = 256

            def body(i, carry, c_ref=c_ref, n_ref=n_ref, rows=rows, w=w, s=s):
                r0 = pl.multiple_of(i * chunk, chunk)
                x = c_ref[s, pl.ds(r0, chunk), :]
                shifted = pltpu.roll(x, w - 4, axis=1)
                new = new_ref[pl.ds(pl.multiple_of(rows[0] + r0, chunk), chunk), :]
                new = pltpu.roll(new, 124 - 4 * nls[s], axis=1)
                if w > 128:
                    n_ref[s, pl.ds(r0, chunk), 0:w - 128] = shifted[:, 0:w - 128]
                sel = jnp.broadcast_to(lane_t[0:1] >= 124, (chunk, 128))
                n_ref[s, pl.ds(r0, chunk), w - 128:w] = jnp.where(sel, new, shifted[:, w - 128:w])
                return carry

            lax.fori_loop(0, nrows // chunk, body, 0)


def _attention_sample(qbd, kvt_new, ca, cb1, cb2, cb3, sinks):
    nseq = ca.shape[0]
    assert nseq % SAMPLE_TILE_SEQS == 0 and kvt_new.shape[1] == 4 * nseq
    assert (ca.shape[2], cb1.shape[2], cb2.shape[2], cb3.shape[2]) == _UNIT_W[1:]
    bias, slope, sink = _sample_tables(sinks)
    ns = SAMPLE_STEP_SEQS
    assert SAMPLE_TILE_SEQS % ns == 0
    cache_spec = lambda a: pl.BlockSpec((ns,) + a.shape[1:], lambda i: (i, 0, 0))
    in_specs = [
        pl.BlockSpec((16 * ns, 1280), lambda i: (i, 0)),
        pl.BlockSpec((kvt_new.shape[0], 128), lambda i: (0, (i * ns) // SAMPLE_TILE_SEQS)),
        cache_spec(ca), cache_spec(cb1), cache_spec(cb2), cache_spec(cb3),
        _const_spec(bias.shape), _const_spec(slope.shape), _const_spec(sink.shape),
    ]
    out_shape = tuple(jax.ShapeDtypeStruct(a.shape, F32) for a in (ca, cb1, cb2, cb3)) + (
        jax.ShapeDtypeStruct((nseq, 8, 768), F32),)
    out_specs = tuple(cache_spec(a) for a in (ca, cb1, cb2, cb3)) + (
        pl.BlockSpec((ns, 8, 768), lambda i: (i, 0, 0)),)
    return pl.pallas_call(
        _k2s_kernel,
        grid=(nseq // ns,),
        in_specs=in_specs,
        out_specs=out_specs,
        out_shape=out_shape,
        compiler_params=pltpu.CompilerParams(
            dimension_semantics=("arbitrary",), vmem_limit_bytes=VMEM_LIMIT),
        name="attention_sample",
    )(qbd, kvt_new, ca, cb1, cb2, cb3, bias, slope, sink)


def _to_feature_major(cache):
    n, w = cache.shape[0], cache.shape[1]
    return jnp.transpose(cache, (0, 2, 3, 4, 1)).reshape(n, -1, w)


def _from_feature_major(c, heads):
    n, _, w = c.shape
    return jnp.transpose(c.reshape(n, 2, heads, HEAD_DIM, w), (0, 4, 1, 2, 3))[None]


def kernel(x_prompt, x_sample, cache_a_kv, cache_b1_kv, cache_b2_kv, cache_b3_kv, state_conv,
           w_in, sinks_a, w_branch_a, w_branch_b, w_out, norm_mix_pre, norm_mix_post,
           norm_ffn_pre, norm_ffn_post, w_up, conv_w, conv_b, w_down):
    assert w_in.shape[0] == 1, "single layer"
    w_rm, w_kvt = _prep_in_weights(w_in[0])
    wba, wbb, wout = (w_branch_a[0].astype(BF16), w_branch_b[0].astype(BF16), w_out[0].astype(BF16))
    wup, wdown = w_up[0].astype(BF16), w_down[0].astype(BF16)
    g_pre, g_post, gf_pre, gf_post = norm_mix_pre, norm_mix_post, norm_ffn_pre, norm_ffn_post
    cw, cb = conv_w[0], conv_b

    qa, kva, qkv1, gates, qkv2, qkv3, at, b1t, b2t, b3t = _in_proj_prompt(x_prompt, g_pre, w_rm, w_kvt)
    oa, og1, og2, og3 = _attention_prompt(sinks_a, qa, kva, qkv1, qkv2, qkv3)
    x1 = _out_proj_prompt(oa, og1, og2, og3, gates, x_prompt, wba, wbb, wout, g_post)
    y_prompt, tail = _ffn_prompt(x1, gf_pre, gf_post, wup, cw, cb, wdown)
    new_conv_p = tail[:, 6:8, :][None]

    nseq, tnew, d = x_sample.shape
    assert tnew == 4
    xs = x_sample.reshape(nseq * tnew, d)
    qbd, kvt_new, gates_s = _in_proj_sample(xs, g_pre, w_rm, w_kvt)
    ca, cb1, cb2, cb3 = (_to_feature_major(c[0]) for c in
                         (cache_a_kv, cache_b1_kv, cache_b2_kv, cache_b3_kv))
    na, nb1, nb2, nb3, o_s = _attention_sample(qbd, kvt_new, ca, cb1, cb2, cb3, sinks_a[0])
    o_s = o_s[:, 0:4, :].reshape(nseq * tnew, 768)
    x1s = _out_proj_sample(o_s, gates_s, xs, wba, wbb, wout, g_post)
    st = state_conv[0]
    zero = jnp.zeros_like(st[:, 0])
    e1 = jnp.stack([st[:, 1], zero, zero, zero], axis=1).reshape(nseq * tnew, D_FF)
    e2 = jnp.stack([st[:, 0], st[:, 1], zero, zero], axis=1).reshape(nseq * tnew, D_FF)
    ys, a_s = _ffn_sample(x1s, e1, e2, gf_pre, gf_post, wup, cw, cb, wdown)
    y_sample = ys.reshape(nseq, tnew, d)
    new_conv_s = a_s.reshape(nseq, tnew, D_FF)[:, 2:4, :][None]

    return (y_prompt, y_sample,
            _from_feature_major(at, A_KV_HEADS), _from_feature_major(na, A_KV_HEADS),
            _from_feature_major(b1t, 4), _from_feature_major(nb1, 4),
            _from_feature_major(b2t, 4), _from_feature_major(nb2, 4),
            _from_feature_major(b3t, 4), _from_feature_major(nb3, 4),
            new_conv_p, new_conv_s)
```
